```python
import math
import jax, jax.numpy as jnp
from jax import lax
import numpy as np

D_MODEL = 1024
BATCH = 2
SEQ = 16384
DEPTH = 2

GRID_W = 64
CTX_LEN = 256
N_MIXERS = 2
EXPAND = 2
E_CONV = EXPAND * D_MODEL
CONV_W = 3
E_SSM = EXPAND * D_MODEL
SSM_GROUP = 16
N_GROUPS = E_SSM // SSM_GROUP
SSM_STATE = 64
SCAN_CHUNK = 128
LN_EPS = 1e-5
DN_ALPHA = (2 * DEPTH) ** 0.25
DN_BETA = (8 * DEPTH) ** -0.25
N_CONV_LAYERS = (DEPTH + 1) // 2
N_SSM_LAYERS = DEPTH // 2

kernel_name = "hybrid_shortconv_s5_prefix_dit"


def _layernorm(x, g, b):
    xf = x.astype(jnp.float32)
    mu = jnp.mean(xf, axis=-1, keepdims=True)
    var = jnp.mean(jnp.square(xf - mu), axis=-1, keepdims=True)
    return ((xf - mu) * lax.rsqrt(var + LN_EPS) * g.astype(jnp.float32) + b.astype(jnp.float32)).astype(x.dtype)


def _ada(cvec, w, b):
    m = jax.nn.silu(cvec) @ w + b
    shift, scale, gate = jnp.split(m, 3, axis=-1)
    return shift[..., None, :], scale[..., None, :], gate[..., None, :]


def _shift_conv(u, w, axis):
    n = u.shape[axis]
    pad = [(0, 0)] * u.ndim
    pad[axis] = (1, 1)
    up = jnp.pad(u, pad)
    sl = lambda k: lax.slice_in_dim(up, k, k + n, axis=axis)
    return sl(0) * w[0] + sl(1) * w[1] + sl(2) * w[2]


def _conv_grid(u, w):
    bsz, L, E = u.shape
    rows = L // GRID_W
    half = E // 2
    ug = u.reshape(bsz, rows, GRID_W, E)
    yh = _shift_conv(ug[..., :half], w[:, :half], axis=2)
    yv = _shift_conv(ug[..., half:], w[:, half:], axis=1)
    return jnp.concatenate([yh, yv], axis=-1).reshape(bsz, L, E)


def _conv_mixer(h, w_in, w_conv, w_out, grid):
    bg, cg, v, z = jnp.split(h @ w_in, 4, axis=-1)
    u = cg * v
    yc = _conv_grid(u, w_conv) if grid else _shift_conv(u, w_conv, axis=1)
    return (bg * yc * jax.nn.silu(z)) @ w_out


def _zoh(lam_re, lam_im, log_step, b_re, b_im):
    dt = jnp.exp(log_step)[:, None]
    mag = jnp.exp(lam_re * dt)
    ar = mag * jnp.cos(lam_im * dt)
    ai = mag * jnp.sin(lam_im * dt)
    qr, qi = ar - 1.0, ai
    den = lam_re * lam_re + lam_im * lam_im
    fr = (qr * lam_re + qi * lam_im) / den
    fi = (qi * lam_re - qr * lam_im) / den
    bbr = fr[..., None] * b_re - fi[..., None] * b_im
    bbi = fr[..., None] * b_im + fi[..., None] * b_re
    return ar, ai, bbr, bbi


def _binop(e1, e2):
    a1r, a1i, b1r, b1i = e1
    a2r, a2i, b2r, b2i = e2
    return (a2r * a1r - a2i * a1i,
            a2r * a1i + a2i * a1r,
            a2r * b1r - a2i * b1i + b2r,
            a2r * b1i + a2i * b1r + b2i)


def _s5_scan(u, h0r, h0i, ar, ai, bbr, bbi, c_re, c_im, with_output):
    bsz, L, _ = u.shape
    n_blk = L // SCAN_CHUNK
    ub = u.reshape(bsz, n_blk, SCAN_CHUNK, N_GROUPS, SSM_GROUP).transpose(1, 0, 2, 3, 4)

    def step(carry, u_blk):
        hr, hi = carry
        bur = jnp.einsum('btgp,gnp->btgn', u_blk, bbr)
        bui = jnp.einsum('btgp,gnp->btgn', u_blk, bbi)
        bur = bur.at[:, 0].add(ar * hr - ai * hi)
        bui = bui.at[:, 0].add(ar * hi + ai * hr)
        a_r = jnp.broadcast_to(ar, bur.shape)
        a_i = jnp.broadcast_to(ai, bur.shape)
        _, _, sr, si = lax.associative_scan(_binop, (a_r, a_i, bur, bui), axis=1)
        new = (sr[:, -1], si[:, -1])
        if with_output:
            y = jnp.einsum('btgn,gpn->btgp', sr, c_re) - jnp.einsum('btgn,gpn->btgp', si, c_im)
            return new, y
        return new, None

    h_final, ys = lax.scan(step, (h0r, h0i), ub)
    if with_output:
        ys = ys.transpose(1, 0, 2, 3, 4).reshape(bsz, L, E_SSM)
    return ys, h_final


def _glu_gate_out(y, z, w_glu, b_glu, w_out):
    g = jax.nn.gelu(y)
    g = g * jax.nn.sigmoid(g @ w_glu + b_glu)
    return (g * jax.nn.silu(z)) @ w_out


def _s5_mixer(h_lat, h_ctx, w_in, lam_re, lam_im, log_step, b_re, b_im, c_re, c_im, d,
              w_glu, b_glu, w_out, ctx_out):
    u_l, z_l = jnp.split(h_lat @ w_in, 2, axis=-1)
    pc = h_ctx @ w_in
    u_c = pc[..., :E_SSM]
    y_l = d * u_l
    y_c = d * u_c if ctx_out else None
    for r in range(2):
        ar, ai, bbr, bbi = _zoh(lam_re[r], lam_im[r], log_step[r], b_re[r], b_im[r])
        seq = (lambda t: t[:, ::-1]) if r == 1 else (lambda t: t)
        dtype = jnp.result_type(u_c.dtype, bbr.dtype)
        h0 = jnp.zeros((u_c.shape[0], N_GROUPS, SSM_STATE), dtype)
        yc, hc = _s5_scan(seq(u_c), h0, h0, ar, ai, bbr, bbi, c_re[r], c_im[r], ctx_out)
        yl, _ = _s5_scan(seq(u_l), hc[0], hc[1], ar, ai, bbr, bbi, c_re[r], c_im[r], True)
        y_l = y_l + seq(yl)
        if ctx_out:
            y_c = y_c + seq(yc)
    out_l = _glu_gate_out(y_l, z_l, w_glu, b_glu, w_out)
    out_c = _glu_gate_out(y_c, pc[..., E_SSM:], w_glu, b_glu, w_out) if ctx_out else None
    return out_l, out_c


def setup_inputs(seed: int = 0) -> dict:
    key = jax.random.key(seed)
    ks = jax.random.split(key, 24)
    f32 = jnp.float32
    nrm = lambda k, shape, s: jax.random.normal(k, shape, f32) * s
    nA, nB = N_CONV_LAYERS, N_SSM_LAYERS
    G, N, P = N_GROUPS, SSM_STATE, SSM_GROUP
    return {
        "x": nrm(ks[0], (BATCH, SEQ, D_MODEL), 1.0),
        "c": nrm(ks[1], (BATCH, D_MODEL), 1.0),
        "ctx": nrm(ks[2], (BATCH, CTX_LEN, D_MODEL), 1.0),
        "c_ctx": nrm(ks[3], (D_MODEL,), 1.0),
        "ada_w": nrm(ks[4], (DEPTH, D_MODEL, 3 * D_MODEL), D_MODEL ** -0.5),
        "ada_b": nrm(ks[5], (DEPTH, 3 * D_MODEL), 0.02),
        "ln_g": 1.0 + nrm(ks[6], (DEPTH, D_MODEL), 0.02),
        "ln_b": nrm(ks[7], (DEPTH, D_MODEL), 0.02),
        "conv_w_in": nrm(ks[8], (nA, D_MODEL, 4 * E_CONV), D_MODEL ** -0.5),
        "conv_w": nrm(ks[9], (nA, CONV_W, E_CONV), CONV_W ** -0.5),
        "conv_w_out": nrm(ks[10], (nA, E_CONV, D_MODEL), DN_BETA * E_CONV ** -0.5),
        "ssm_w_in": nrm(ks[11], (nB, D_MODEL, 2 * E_SSM), D_MODEL ** -0.5),
        "ssm_lam_re": -0.5 * jnp.exp(nrm(ks[12], (nB, 2, G, N), 0.05)),
        "ssm_lam_im": jnp.pi * jnp.arange(N, dtype=f32) + nrm(ks[13], (nB, 2, G, N), 0.05),
        "ssm_log_step": jax.random.uniform(ks[14], (nB, 2, G), f32, math.log(1e-3), math.log(1e-1)),
        "ssm_b_re": nrm(ks[15], (nB, 2, G, N, P), (2 * P) ** -0.5),
        "ssm_b_im": nrm(ks[16], (nB, 2, G, N, P), (2 * P) ** -0.5),
        "ssm_c_re": nrm(ks[17], (nB, 2, G, P, N), N ** -0.5),
        "ssm_c_im": nrm(ks[18], (nB, 2, G, P, N), N ** -0.5),
        "ssm_d": nrm(ks[19], (nB, E_SSM), 1.0),
        "ssm_w_glu": nrm(ks[20], (nB, E_SSM, E_SSM), E_SSM ** -0.5),
        "ssm_b_glu": nrm(ks[21], (nB, E_SSM), 0.02),
        "ssm_w_out": nrm(ks[22], (nB, E_SSM, D_MODEL), DN_BETA * E_SSM ** -0.5),
    }


def reference(x, c, ctx, c_ctx, ada_w, ada_b, ln_g, ln_b, conv_w_in, conv_w, conv_w_out,
              ssm_w_in, ssm_lam_re, ssm_lam_im, ssm_log_step, ssm_b_re, ssm_b_im,
              ssm_c_re, ssm_c_im, ssm_d, ssm_w_glu, ssm_b_glu, ssm_w_out):
    for i in range(DEPTH):
        last = i == DEPTH - 1
        is_conv = (i % N_MIXERS) == 0
        j = i // N_MIXERS
        need_ctx_out = not last
        need_ctx_in = need_ctx_out or not is_conv
        sh, sc, gt = _ada(c, ada_w[i], ada_b[i])
        hx = x * (1.0 + sc) + sh
        if need_ctx_in:
            sh_c, sc_c, gt_c = _ada(c_ctx, ada_w[i], ada_b[i])
            hc = ctx * (1.0 + sc_c) + sh_c
        if is_conv:
            fx = _conv_mixer(hx, conv_w_in[j], conv_w[j], conv_w_out[j], True)
            fc = _conv_mixer(hc, conv_w_in[j], conv_w[j], conv_w_out[j], False) if need_ctx_out else None
        else:
            fx, fc = _s5_mixer(hx, hc, ssm_w_in[j], ssm_lam_re[j], ssm_lam_im[j], ssm_log_step[j],
                               ssm_b_re[j], ssm_b_im[j], ssm_c_re[j], ssm_c_im[j], ssm_d[j],
                               ssm_w_glu[j], ssm_b_glu[j], ssm_w_out[j], need_ctx_out)
        x = _layernorm(DN_ALPHA * x + gt * fx, ln_g[i], ln_b[i])
        if need_ctx_out:
            ctx = _layernorm(DN_ALPHA * ctx + gt_c * fc, ln_g[i], ln_b[i])
    return x
```

```python
import functools
import math

import jax
import jax.numpy as jnp
import numpy as np
from jax import lax
from jax.experimental import pallas as pl
from jax.experimental.pallas import tpu as pltpu

D_MODEL = 1024
GRID_W = 64
E_CONV = 2048
E_SSM = 2048
SSM_GROUP = 16
N_GROUPS = E_SSM // SSM_GROUP
SSM_STATE = 64
LN_EPS = 1e-5
DEPTH = 2
DN_ALPHA = (2 * DEPTH) ** 0.25

T_CHUNK = 16
N_SEG = 8
TP = T_CHUNK * SSM_GROUP
VMEM_LIMIT = 56 * 1024 * 1024

_F32 = jnp.float32
_BF16 = jnp.bfloat16


def _dot(a, b):
    return jnp.dot(a, b, preferred_element_type=_F32)


def _silu(z):
    return z * jax.nn.sigmoid(z)


def _gelu_tanh(x):
    c = math.sqrt(2.0 / math.pi)
    return x * (0.5 * (1.0 + jnp.tanh(c * (x + 0.044715 * (x * x * x)))))


def _layernorm_rows(r, g, b):
    mu = jnp.mean(r, axis=-1, keepdims=True)
    d = r - mu
    var = jnp.mean(d * d, axis=-1, keepdims=True)
    return d * lax.rsqrt(var + LN_EPS) * g + b


def _const_spec(shape):
    nd = len(shape)
    return pl.BlockSpec(shape, lambda *_: (0,) * nd, pipeline_mode=pl.Buffered(1))


def _ada_body(c_ref, w_ref, b_ref, o_ref):
    o_ref[0] = jnp.dot(_silu(c_ref[...]), w_ref[0], precision=lax.Precision.HIGHEST,
                       preferred_element_type=_F32) + b_ref[0]


def _ada_call(cs, ada_w, ada_b):
    depth = ada_w.shape[0]
    rows = cs.shape[0]
    return pl.pallas_call(
        _ada_body,
        grid=(depth, 3),
        in_specs=[
            pl.BlockSpec((rows, D_MODEL), lambda i, j: (0, 0)),
            pl.BlockSpec((1, D_MODEL, D_MODEL), lambda i, j: (i, 0, j)),
            pl.BlockSpec((1, 1, D_MODEL), lambda i, j: (i, 0, j)),
        ],
        out_specs=pl.BlockSpec((1, rows, D_MODEL), lambda i, j: (i, 0, j)),
        out_shape=jax.ShapeDtypeStruct((depth, rows, 3 * D_MODEL), _F32),
        compiler_params=pltpu.CompilerParams(dimension_semantics=("arbitrary", "arbitrary")),
        name="ada_mod",
    )(cs, ada_w, ada_b.reshape(depth, 1, 3 * D_MODEL))


CONV_CK = 512


def _conv_layer_body(*refs, tb, period, n_horizontal, blocks_per_batch, halo):
    if halo:
        x_ref, xp_ref, xn_ref, mod_ref, win_ref, cw_ref, wout_ref, g_ref, b_ref, o_ref = refs
    else:
        x_ref, mod_ref, win_ref, cw_ref, wout_ref, g_ref, b_ref, o_ref = refs
    sh = mod_ref[0, 0:1, :]
    sc1 = 1.0 + mod_ref[0, 1:2, :]
    gt = mod_ref[0, 2:3, :]
    x = x_ref[...]
    h = (x * sc1 + sh).astype(_BF16)
    if halo:
        i = pl.program_id(0)
        jb = i % blocks_per_batch
        prev_ok = (jb != 0).astype(_F32)
        next_ok = (jb != blocks_per_batch - 1).astype(_F32)
        hp = (xp_ref[...] * sc1 + sh).astype(_BF16)
        hn = (xn_ref[...] * sc1 + sh).astype(_BF16)
    pos = lax.broadcasted_iota(jnp.int32, (tb, CONV_CK), 0) % period
    acc = jnp.zeros((tb, D_MODEL), _F32)
    for k in range(E_CONV // CONV_CK):
        c0 = k * CONV_CK
        bg = _dot(h, win_ref[:, c0:c0 + CONV_CK])
        u = _dot(h, win_ref[:, E_CONV + c0:E_CONV + c0 + CONV_CK]) * \
            _dot(h, win_ref[:, 2 * E_CONV + c0:2 * E_CONV + c0 + CONV_CK])
        z = _dot(h, win_ref[:, 3 * E_CONV + c0:3 * E_CONV + c0 + CONV_CK])
        w0 = cw_ref[0:1, c0:c0 + CONV_CK]
        w1 = cw_ref[1:2, c0:c0 + CONV_CK]
        w2 = cw_ref[2:3, c0:c0 + CONV_CK]
        if c0 < n_horizontal:
            um = jnp.where(pos != 0, pltpu.roll(u, 1, 0), 0.0)
            up = jnp.where(pos != period - 1, pltpu.roll(u, tb - 1, 0), 0.0)
        else:
            uph = _dot(hp, win_ref[:, E_CONV + c0:E_CONV + c0 + CONV_CK]) * \
                _dot(hp, win_ref[:, 2 * E_CONV + c0:2 * E_CONV + c0 + CONV_CK]) * prev_ok
            unh = _dot(hn, win_ref[:, E_CONV + c0:E_CONV + c0 + CONV_CK]) * \
                _dot(hn, win_ref[:, 2 * E_CONV + c0:2 * E_CONV + c0 + CONV_CK]) * next_ok
            um = jnp.concatenate([uph, u[:tb - GRID_W]], axis=0)
            up = jnp.concatenate([u[GRID_W:], unh], axis=0)
        yc = um * w0 + u * w1 + up * w2
        gated = (bg * yc * _silu(z)).astype(_BF16)
        acc = acc + _dot(gated, wout_ref[c0:c0 + CONV_CK, :])
    r = DN_ALPHA * x + gt * acc
    o_ref[...] = _layernorm_rows(r, g_ref[...], b_ref[...])


def _conv_layer_call(x2d, mod3, w_in, conv_w, w_out, ln_g, ln_b, *, tb, tokens_per_batch,
                     period, n_horizontal, halo, mod_per_batch, name):
    nt = x2d.shape[0]
    bpb = tokens_per_batch // tb
    hb = tb // GRID_W
    n_hblocks = nt // GRID_W
    mod_map = (lambda i: (i // bpb, 0, 0)) if mod_per_batch else (lambda i: (0, 0, 0))
    in_specs = [pl.BlockSpec((tb, D_MODEL), lambda i: (i, 0))]
    args = [x2d]
    if halo:
        in_specs += [
            pl.BlockSpec((GRID_W, D_MODEL), lambda i: (jnp.maximum(i * hb - 1, 0), 0)),
            pl.BlockSpec((GRID_W, D_MODEL), lambda i: (jnp.minimum((i + 1) * hb, n_hblocks - 1), 0)),
        ]
        args += [x2d, x2d]
    in_specs += [
        pl.BlockSpec((1, 3, D_MODEL), mod_map),
        _const_spec((D_MODEL, 4 * E_CONV)),
        _const_spec((3, E_CONV)),
        _const_spec((E_CONV, D_MODEL)),
        _const_spec((1, D_MODEL)),
        _const_spec((1, D_MODEL)),
    ]
    args += [mod3, w_in, conv_w, w_out, ln_g, ln_b]
    body = functools.partial(_conv_layer_body, tb=tb, period=period, n_horizontal=n_horizontal,
                             blocks_per_batch=bpb, halo=halo)
    return pl.pallas_call(
        body,
        grid=(nt // tb,),
        in_specs=in_specs,
        out_specs=pl.BlockSpec((tb, D_MODEL), lambda i: (i, 0)),
        out_shape=jax.ShapeDtypeStruct((nt, D_MODEL), _F32),
        compiler_params=pltpu.CompilerParams(dimension_semantics=("arbitrary",),
                                             vmem_limit_bytes=VMEM_LIMIT),
        name=name,
    )(*args)


def _s5_inproj_body(x_ref, mod_ref, w_ref, o_ref):
    sh = mod_ref[0, 0:1, :]
    sc1 = 1.0 + mod_ref[0, 1:2, :]
    h = (x_ref[...] * sc1 + sh).astype(_BF16)
    o_ref[...] = _dot(h, w_ref[...]).astype(_BF16)


def _s5_inproj_call(x2d, mod3, w_u, *, tb, tokens_per_batch, mod_per_batch, name):
    nt = x2d.shape[0]
    bpb = tokens_per_batch // tb
    mod_map = (lambda i: (i // bpb, 0, 0)) if mod_per_batch else (lambda i: (0, 0, 0))
    return pl.pallas_call(
        _s5_inproj_body,
        grid=(nt // tb,),
        in_specs=[
            pl.BlockSpec((tb, D_MODEL), lambda i: (i, 0)),
            pl.BlockSpec((1, 3, D_MODEL), mod_map),
            _const_spec((D_MODEL, E_SSM)),
        ],
        out_specs=pl.BlockSpec((tb, E_SSM), lambda i: (i, 0)),
        out_shape=jax.ShapeDtypeStruct((nt, E_SSM), _BF16),
        compiler_params=pltpu.CompilerParams(dimension_semantics=("arbitrary",),
                                             vmem_limit_bytes=VMEM_LIMIT),
        name=name,
    )(x2d, mod3, w_u)


def _cmul_add(ar, ai, hr, hi, sr, si):
    return ar * hr - ai * hi + sr, ar * hi + ai * hr + si


def _s5_core_body(u_ref, uc_ref, m_ref, bm_ref, cm_ref, coef_ref, y_ref,
                  s_ref, h_ref, f_ref, gf_ref, gb_ref, *, n_cl, n_ctx_chunks, n_batch):
    half = SSM_STATE
    rows = 2 * N_SEG
    bm = bm_ref[0]
    s_ref[...] = _dot(u_ref[0], bm)
    sc = _dot(uc_ref[0], bm)
    ar = coef_ref[0, 0:1, :]
    ai = coef_ref[0, 1:2, :]
    pr = coef_ref[0, 2:3, :]
    pi = coef_ref[0, 3:4, :]

    is_fwd8 = lax.broadcasted_iota(jnp.int32, (8, 2 * half), 1) < half
    hr = jnp.zeros((8, 2 * half), _F32)
    hi = jnp.zeros((8, 2 * half), _F32)
    for k in range(n_ctx_chunks):
        kb = n_ctx_chunks - 1 - k
        sr = jnp.where(is_fwd8, sc[k * 8:(k + 1) * 8, 0:2 * half], sc[kb * 8:(kb + 1) * 8, 0:2 * half])
        si = jnp.where(is_fwd8, sc[k * 8:(k + 1) * 8, 2 * half:], sc[kb * 8:(kb + 1) * 8, 2 * half:])
        hr, hi = _cmul_add(ar, ai, hr, hi, sr, si)
    ctx_r, ctx_i = hr, hi

    is_fwd = lax.broadcasted_iota(jnp.int32, (rows, 2 * half), 1) < half

    def load_s(k):
        kf = pl.multiple_of(k * rows, rows)
        kb = pl.multiple_of((n_cl - 1 - k) * rows, rows)
        sf = s_ref[pl.ds(kf, rows), :]
        sb = s_ref[pl.ds(kb, rows), :]
        return (jnp.where(is_fwd, sf[:, 0:2 * half], sb[:, 0:2 * half]),
                jnp.where(is_fwd, sf[:, 2 * half:], sb[:, 2 * half:]))

    def pass1(k, carry):
        sr, si = load_s(k)
        return _cmul_add(ar, ai, carry[0], carry[1], sr, si)

    zero = jnp.zeros((rows, 2 * half), _F32)
    f_r, f_i = lax.fori_loop(0, n_cl, pass1, (zero, zero))
    f_ref[0] = f_r
    f_ref[1] = f_i

    for b in range(n_batch):
        gr = ctx_r[b:b + 1, :]
        gi = ctx_i[b:b + 1, :]
        gf_ref[0, b * N_SEG:b * N_SEG + 1, :] = gr
        gf_ref[1, b * N_SEG:b * N_SEG + 1, :] = gi
        for s in range(1, N_SEG):
            row = b * N_SEG + s
            gr, gi = _cmul_add(pr, pi, gr, gi, f_ref[0, row - 1:row, :], f_ref[1, row - 1:row, :])
            gf_ref[0, row:row + 1, :] = gr
            gf_ref[1, row:row + 1, :] = gi
        gr = ctx_r[b:b + 1, :]
        gi = ctx_i[b:b + 1, :]
        last = b * N_SEG + N_SEG - 1
        gb_ref[0, last:last + 1, :] = gr
        gb_ref[1, last:last + 1, :] = gi
        for s in range(N_SEG - 2, -1, -1):
            row = b * N_SEG + s
            gr, gi = _cmul_add(pr, pi, gr, gi, f_ref[0, row + 1:row + 2, :], f_ref[1, row + 1:row + 2, :])
            gb_ref[0, row:row + 1, :] = gr
            gb_ref[1, row:row + 1, :] = gi

    def pass2(k, carry):
        hr, hi = carry
        kf = pl.multiple_of(k * rows, rows)
        kb = pl.multiple_of((n_cl - 1 - k) * rows, rows)
        h_ref[pl.ds(kf, rows), 0:half] = hr[:, 0:half]
        h_ref[pl.ds(kf, rows), 2 * half:3 * half] = hi[:, 0:half]
        h_ref[pl.ds(kb, rows), half:2 * half] = hr[:, half:]
        h_ref[pl.ds(kb, rows), 3 * half:] = hi[:, half:]
        sr, si = load_s(k)
        return _cmul_add(ar, ai, hr, hi, sr, si)

    h0r = jnp.where(is_fwd, gf_ref[0], gb_ref[0])
    h0i = jnp.where(is_fwd, gf_ref[1], gb_ref[1])
    lax.fori_loop(0, n_cl, pass2, (h0r, h0i))

    y_ref[0] = _dot(u_ref[0], m_ref[0]) + _dot(h_ref[...].astype(_BF16), cm_ref[0])


def _s5_core_call(ug, ucg, m_op, b_op, c_op, coef, *, n_cl, n_ctx_chunks, n_batch):
    g, nrow, _ = ug.shape
    ncrow = ucg.shape[1]
    rows = 2 * N_SEG
    body = functools.partial(_s5_core_body, n_cl=n_cl, n_ctx_chunks=n_ctx_chunks, n_batch=n_batch)
    gspec = lambda shape: pl.BlockSpec((1,) + shape, lambda i: (i, 0, 0))
    return pl.pallas_call(
        body,
        grid=(g,),
        in_specs=[gspec((nrow, TP)), gspec((ncrow, TP)), gspec((TP, TP)), gspec((TP, 4 * SSM_STATE)),
                  gspec((4 * SSM_STATE, TP)), gspec((8, 2 * SSM_STATE))],
        out_specs=gspec((nrow, TP)),
        out_shape=jax.ShapeDtypeStruct((g, nrow, TP), _F32),
        scratch_shapes=[
            pltpu.VMEM((nrow, 4 * SSM_STATE), _F32),
            pltpu.VMEM((nrow, 4 * SSM_STATE), _F32),
            pltpu.VMEM((2, rows, 2 * SSM_STATE), _F32),
            pltpu.VMEM((2, rows, 2 * SSM_STATE), _F32),
            pltpu.VMEM((2, rows, 2 * SSM_STATE), _F32),
        ],
        compiler_params=pltpu.CompilerParams(dimension_semantics=("arbitrary",),
                                             vmem_limit_bytes=VMEM_LIMIT),
        name="s5_core",
    )(ug, ucg, m_op, b_op, c_op, coef)


def _s5_out_body(x_ref, y_ref, mod_ref, wz_ref, wglu_ref, bglu_ref, wout_ref, g_ref, b_ref, o_ref):
    sh = mod_ref[0, 0:1, :]
    sc1 = 1.0 + mod_ref[0, 1:2, :]
    gt = mod_ref[0, 2:3, :]
    x = x_ref[...]
    h = (x * sc1 + sh).astype(_BF16)
    g = _gelu_tanh(y_ref[...])
    gl = _dot(g.astype(_BF16), wglu_ref[...]) + bglu_ref[...]
    g = g * jax.nn.sigmoid(gl)
    z = _dot(h, wz_ref[...])
    o = _dot((g * _silu(z)).astype(_BF16), wout_ref[...])
    r = DN_ALPHA * x + gt * o
    o_ref[...] = _layernorm_rows(r, g_ref[...], b_ref[...])


def _s5_out_call(x2d, y2d, mod3, w_z, w_glu, b_glu, w_out, ln_g, ln_b, *, tb, tokens_per_batch):
    nt = x2d.shape[0]
    bpb = tokens_per_batch // tb
    return pl.pallas_call(
        _s5_out_body,
        grid=(nt // tb,),
        in_specs=[
            pl.BlockSpec((tb, D_MODEL), lambda i: (i, 0)),
            pl.BlockSpec((tb, E_SSM), lambda i: (i, 0)),
            pl.BlockSpec((1, 3, D_MODEL), lambda i: (i // bpb, 0, 0)),
            _const_spec((D_MODEL, E_SSM)),
            _const_spec((E_SSM, E_SSM)),
            _const_spec((1, E_SSM)),
            _const_spec((E_SSM, D_MODEL)),
            _const_spec((1, D_MODEL)),
            _const_spec((1, D_MODEL)),
        ],
        out_specs=pl.BlockSpec((tb, D_MODEL), lambda i: (i, 0)),
        out_shape=jax.ShapeDtypeStruct((nt, D_MODEL), _F32),
        compiler_params=pltpu.CompilerParams(dimension_semantics=("arbitrary",),
                                             vmem_limit_bytes=VMEM_LIMIT),
        name="s5_out",
    )(x2d, y2d, mod3, w_z, w_glu, b_glu, w_out, ln_g, ln_b)


def _s5_operators(lam_re, lam_im, log_step, b_re, b_im, c_re, c_im, d, n_cl):
    t = T_CHUNK
    hi = lax.Precision.HIGHEST
    dt = jnp.exp(log_step)[..., None]
    mag = jnp.exp(lam_re * dt)
    a_r = mag * jnp.cos(lam_im * dt)
    a_i = mag * jnp.sin(lam_im * dt)
    q_r, q_i = a_r - 1.0, a_i
    den = lam_re * lam_re + lam_im * lam_im
    f_r = (q_r * lam_re + q_i * lam_im) / den
    f_i = (q_i * lam_re - q_r * lam_im) / den
    bb_r = f_r[..., None] * b_re - f_i[..., None] * b_im
    bb_i = f_r[..., None] * b_im + f_i[..., None] * b_re

    pw_r = [jnp.ones_like(a_r)]
    pw_i = [jnp.zeros_like(a_i)]
    for _ in range(t):
        pr_, pi_ = pw_r[-1], pw_i[-1]
        pw_r.append(pr_ * a_r - pi_ * a_i)
        pw_i.append(pr_ * a_i + pi_ * a_r)
    pw_r = jnp.stack(pw_r)
    pw_i = jnp.stack(pw_i)
    s_r, s_i = pw_r[t], pw_i[t]
    for _ in range(int(round(math.log2(n_cl)))):
        s_r, s_i = s_r * s_r - s_i * s_i, 2.0 * s_r * s_i

    ab_r = pw_r[..., None] * bb_r - pw_i[..., None] * bb_i
    ab_i = pw_r[..., None] * bb_i + pw_i[..., None] * bb_r
    ca_r = c_re * pw_r[:, :, :, None, :] - c_im * pw_i[:, :, :, None, :]
    ca_i = c_re * pw_i[:, :, :, None, :] + c_im * pw_r[:, :, :, None, :]

    kk = (jnp.einsum('rgpn,trgnq->trgpq', c_re, ab_r[:t], precision=hi)
          - jnp.einsum('rgpn,trgnq->trgpq', c_im, ab_i[:t], precision=hi))
    kk = jnp.concatenate([kk, jnp.zeros_like(kk[:1])], axis=0)
    s_idx = np.arange(t)[:, None]
    t_idx = np.arange(t)[None, :]
    idx_f = np.where(t_idx >= s_idx, t_idx - s_idx, t)
    idx_b = np.where(s_idx >= t_idx, s_idx - t_idx, t)
    m_f = kk[idx_f, 0]
    m_b = kk[idx_b, 1]
    eye_t = jnp.eye(t, dtype=_F32)[:, :, None, None, None]
    eye_p = jnp.eye(SSM_GROUP, dtype=_F32)[None, None, None]
    skip = eye_t * eye_p * d.reshape(N_GROUPS, SSM_GROUP)[None, None, :, :, None]
    m_op = (m_f + m_b + skip).transpose(2, 0, 4, 1, 3).reshape(N_GROUPS, TP, TP)

    def bpart(x, r, rev):
        x = x[:t, r]
        if rev:
            x = x[::-1]
        return x.transpose(1, 0, 3, 2).reshape(N_GROUPS, TP, SSM_STATE)
    b_op = jnp.concatenate([bpart(ab_r, 0, True), bpart(ab_r, 1, False),
                            bpart(ab_i, 0, True), bpart(ab_i, 1, False)], axis=-1)

    def cpart(x, r, rev):
        x = x[1:, r]
        if rev:
            x = x[::-1]
        return x.transpose(1, 3, 0, 2).reshape(N_GROUPS, SSM_STATE, TP)
    c_op = jnp.concatenate([cpart(ca_r, 0, False), cpart(ca_r, 1, True),
                            -cpart(ca_i, 0, False), -cpart(ca_i, 1, True)], axis=1)

    cat = lambda x: jnp.concatenate([x[0], x[1]], axis=-1)
    coef = jnp.stack([cat(pw_r[t]), cat(pw_i[t]), cat(s_r), cat(s_i)], axis=1)
    coef = jnp.concatenate([coef, jnp.zeros((N_GROUPS, 4, 2 * SSM_STATE), _F32)], axis=1)
    return m_op.astype(_BF16), b_op.astype(_BF16), c_op.astype(_BF16), coef


def kernel(x, c, ctx, c_ctx, ada_w, ada_b, ln_g, ln_b, conv_w_in, conv_w, conv_w_out,
           ssm_w_in, ssm_lam_re, ssm_lam_im, ssm_log_step, ssm_b_re, ssm_b_im,
           ssm_c_re, ssm_c_im, ssm_d, ssm_w_glu, ssm_b_glu, ssm_w_out):
    bsz, seq, _ = x.shape
    ctx_len = ctx.shape[1]
    assert seq % (T_CHUNK * N_SEG) == 0 and ctx_len % T_CHUNK == 0 and bsz <= 8
    n_cl = seq // (T_CHUNK * N_SEG)
    n_ctx_chunks = ctx_len // T_CHUNK

    cs = jnp.zeros((8, D_MODEL), _F32).at[:bsz].set(c).at[bsz].set(c_ctx)
    mod = _ada_call(cs, ada_w, ada_b).reshape(DEPTH, 8, 3, D_MODEL)
    mod_x = [mod[i, :bsz] for i in range(DEPTH)]
    mod_c = [mod[i, bsz:bsz + 1] for i in range(DEPTH)]

    x2d = x.reshape(bsz * seq, D_MODEL)
    c2d = ctx.reshape(bsz * ctx_len, D_MODEL)
    g0, b0 = ln_g[0:1], ln_b[0:1]
    g1, b1 = ln_g[1:2], ln_b[1:2]

    w_in0 = conv_w_in[0].astype(_BF16)
    w_out0 = conv_w_out[0].astype(_BF16)
    x1 = _conv_layer_call(x2d, mod_x[0], w_in0, conv_w[0], w_out0, g0, b0, tb=512,
                          tokens_per_batch=seq, period=GRID_W, n_horizontal=E_CONV // 2,
                          halo=True, mod_per_batch=True, name="conv_layer_x")
    c1 = _conv_layer_call(c2d, mod_c[0], w_in0, conv_w[0], w_out0, g0, b0, tb=ctx_len,
                          tokens_per_batch=ctx_len, period=ctx_len, n_horizontal=E_CONV,
                          halo=False, mod_per_batch=False, name="conv_layer_ctx")

    w_in1 = ssm_w_in[0].astype(_BF16)
    w_u, w_z = w_in1[:, :E_SSM], w_in1[:, E_SSM:]
    u_x = _s5_inproj_call(x1, mod_x[1], w_u, tb=1024, tokens_per_batch=seq, mod_per_batch=True,
                          name="s5_inproj_x")
    u_c = _s5_inproj_call(c1, mod_c[1], w_u, tb=ctx_len, tokens_per_batch=ctx_len,
                          mod_per_batch=False, name="s5_inproj_ctx")

    m_op, b_op, c_op, coef = _s5_operators(ssm_lam_re[0], ssm_lam_im[0], ssm_log_step[0],
                                           ssm_b_re[0], ssm_b_im[0], ssm_c_re[0], ssm_c_im[0],
                                           ssm_d[0], n_cl)

    ug = u_x.reshape(bsz, N_SEG, n_cl, T_CHUNK, N_GROUPS, SSM_GROUP)
    ug = ug.transpose(4, 2, 0, 1, 3, 5)
    ug = jnp.pad(ug, ((0, 0), (0, 0), (0, 2 - bsz), (0, 0), (0, 0), (0, 0)))
    ug = ug.reshape(N_GROUPS, n_cl * 2 * N_SEG, TP)
    ucg = u_c.reshape(bsz, n_ctx_chunks, T_CHUNK, N_GROUPS, SSM_GROUP).transpose(3, 1, 0, 2, 4)
    ucg = jnp.pad(ucg, ((0, 0), (0, 0), (0, 8 - bsz), (0, 0), (0, 0)))
    ucg = ucg.reshape(N_GROUPS, n_ctx_chunks * 8, TP)

    yg = _s5_core_call(ug, ucg, m_op, b_op, c_op, coef, n_cl=n_cl, n_ctx_chunks=n_ctx_chunks,
                       n_batch=bsz)
    y = yg.reshape(N_GROUPS, n_cl, 2, N_SEG, T_CHUNK, SSM_GROUP)[:, :, :bsz]
    y = y.transpose(2, 3, 1, 4, 0, 5).reshape(bsz * seq, E_SSM)

    out = _s5_out_call(x1, y, mod_x[1], w_z, ssm_w_glu[0].astype(_BF16), ssm_b_glu[0:1],
                       ssm_w_out[0].astype(_BF16), g1, b1, tb=512, tokens_per_batch=seq)
    return out.reshape(bsz, seq, D_MODEL)
```

```python
import functools
import math

import jax
import jax.numpy as jnp
import numpy as np
from jax import lax
from jax.experimental import pallas as pl
from jax.experimental.pallas import tpu as pltpu

D_MODEL = 1024
GRID_W = 64
E_CONV = 2048
E_SSM = 2048
SSM_GROUP = 16
N_GROUPS = E_SSM // SSM_GROUP
SSM_STATE = 64
LN_EPS = 1e-5
DEPTH = 2
DN_ALPHA = (2 * DEPTH) ** 0.25

T_CHUNK = 16
N_SEG = 8
TP = T_CHUNK * SSM_GROUP
LANES = 128
VMEM_LIMIT = 56 * 1024 * 1024

_F32 = jnp.float32
_BF16 = jnp.bfloat16


def _dot(a, b):
    return jnp.dot(a, b, preferred_element_type=_F32)


def _silu(z):
    return z * jax.nn.sigmoid(z)


def _gelu_tanh(x):
    c = math.sqrt(2.0 / math.pi)
    return x * (0.5 * (1.0 + jnp.tanh(c * (x + 0.044715 * (x * x * x)))))


def _layernorm_rows(r, g, b):
    mu = jnp.mean(r, axis=-1, keepdims=True)
    d = r - mu
    var = jnp.mean(d * d, axis=-1, keepdims=True)
    return d * lax.rsqrt(var + LN_EPS) * g + b


def _const_spec(shape):
    nd = len(shape)
    return pl.BlockSpec(shape, lambda *_: (0,) * nd, pipeline_mode=pl.Buffered(1))


def _ada_body(c_ref, w_ref, b_ref, o_ref):
    o_ref[0] = jnp.dot(_silu(c_ref[...]), w_ref[0], precision=lax.Precision.HIGHEST,
                       preferred_element_type=_F32) + b_ref[0]


def _ada_call(cs, ada_w, ada_b):
    depth = ada_w.shape[0]
    rows = cs.shape[0]
    return pl.pallas_call(
        _ada_body,
        grid=(depth, 3),
        in_specs=[
            pl.BlockSpec((rows, D_MODEL), lambda i, j: (0, 0)),
            pl.BlockSpec((1, D_MODEL, D_MODEL), lambda i, j: (i, 0, j)),
            pl.BlockSpec((1, 1, D_MODEL), lambda i, j: (i, 0, j)),
        ],
        out_specs=pl.BlockSpec((1, rows, D_MODEL), lambda i, j: (i, 0, j)),
        out_shape=jax.ShapeDtypeStruct((depth, rows, 3 * D_MODEL), _F32),
        compiler_params=pltpu.CompilerParams(dimension_semantics=("arbitrary", "arbitrary")),
        name="ada_mod",
    )(cs, ada_w, ada_b.reshape(depth, 1, 3 * D_MODEL))


CONV_CK = 512


def _conv_layer_body(*refs, tb, period, n_horizontal, blocks_per_batch, halo):
    if halo:
        x_ref, xp_ref, xn_ref, mod_ref, win_ref, cw_ref, wout_ref, g_ref, b_ref, o_ref = refs
    else:
        x_ref, mod_ref, win_ref, cw_ref, wout_ref, g_ref, b_ref, o_ref = refs
    sh = mod_ref[0, 0:1, :]
    sc1 = 1.0 + mod_ref[0, 1:2, :]
    gt = mod_ref[0, 2:3, :]
    x = x_ref[...]
    h = (x * sc1 + sh).astype(_BF16)
    if halo:
        i = pl.program_id(0)
        jb = i % blocks_per_batch
        prev_ok = (jb != 0).astype(_F32)
        next_ok = (jb != blocks_per_batch - 1).astype(_F32)
        hp = (xp_ref[...] * sc1 + sh).astype(_BF16)
        hn = (xn_ref[...] * sc1 + sh).astype(_BF16)
    pos = lax.broadcasted_iota(jnp.int32, (tb, CONV_CK), 0) % period
    acc = jnp.zeros((tb, D_MODEL), _F32)
    for k in range(E_CONV // CONV_CK):
        c0 = k * CONV_CK
        bg = _dot(h, win_ref[:, c0:c0 + CONV_CK])
        u = _dot(h, win_ref[:, E_CONV + c0:E_CONV + c0 + CONV_CK]) * \
            _dot(h, win_ref[:, 2 * E_CONV + c0:2 * E_CONV + c0 + CONV_CK])
        z = _dot(h, win_ref[:, 3 * E_CONV + c0:3 * E_CONV + c0 + CONV_CK])
        w0 = cw_ref[0:1, c0:c0 + CONV_CK]
        w1 = cw_ref[1:2, c0:c0 + CONV_CK]
        w2 = cw_ref[2:3, c0:c0 + CONV_CK]
        if c0 < n_horizontal:
            um = jnp.where(pos != 0, pltpu.roll(u, 1, 0), 0.0)
            up = jnp.where(pos != period - 1, pltpu.roll(u, tb - 1, 0), 0.0)
        else:
            uph = _dot(hp, win_ref[:, E_CONV + c0:E_CONV + c0 + CONV_CK]) * \
                _dot(hp, win_ref[:, 2 * E_CONV + c0:2 * E_CONV + c0 + CONV_CK]) * prev_ok
            unh = _dot(hn, win_ref[:, E_CONV + c0:E_CONV + c0 + CONV_CK]) * \
                _dot(hn, win_ref[:, 2 * E_CONV + c0:2 * E_CONV + c0 + CONV_CK]) * next_ok
            um = jnp.concatenate([uph, u[:tb - GRID_W]], axis=0)
            up = jnp.concatenate([u[GRID_W:], unh], axis=0)
        yc = um * w0 + u * w1 + up * w2
        gated = (bg * yc * _silu(z)).astype(_BF16)
        acc = acc + _dot(gated, wout_ref[c0:c0 + CONV_CK, :])
    r = DN_ALPHA * x + gt * acc
    o_ref[...] = _layernorm_rows(r, g_ref[...], b_ref[...])


def _conv_layer_call(x2d, mod3, w_in, conv_w, w_out, ln_g, ln_b, *, tb, tokens_per_batch,
                     period, n_horizontal, halo, mod_per_batch, name):
    nt = x2d.shape[0]
    bpb = tokens_per_batch // tb
    hb = tb // GRID_W
    n_hblocks = nt // GRID_W
    mod_map = (lambda i: (i // bpb, 0, 0)) if mod_per_batch else (lambda i: (0, 0, 0))
    in_specs = [pl.BlockSpec((tb, D_MODEL), lambda i: (i, 0))]
    args = [x2d]
    if halo:
        in_specs += [
            pl.BlockSpec((GRID_W, D_MODEL), lambda i: (jnp.maximum(i * hb - 1, 0), 0)),
            pl.BlockSpec((GRID_W, D_MODEL), lambda i: (jnp.minimum((i + 1) * hb, n_hblocks - 1), 0)),
        ]
        args += [x2d, x2d]
    in_specs += [
        pl.BlockSpec((1, 3, D_MODEL), mod_map),
        _const_spec((D_MODEL, 4 * E_CONV)),
        _const_spec((3, E_CONV)),
        _const_spec((E_CONV, D_MODEL)),
        _const_spec((1, D_MODEL)),
        _const_spec((1, D_MODEL)),
    ]
    args += [mod3, w_in, conv_w, w_out, ln_g, ln_b]
    body = functools.partial(_conv_layer_body, tb=tb, period=period, n_horizontal=n_horizontal,
                             blocks_per_batch=bpb, halo=halo)
    return pl.pallas_call(
        body,
        grid=(nt // tb,),
        in_specs=in_specs,
        out_specs=pl.BlockSpec((tb, D_MODEL), lambda i: (i, 0)),
        out_shape=jax.ShapeDtypeStruct((nt, D_MODEL), _F32),
        compiler_params=pltpu.CompilerParams(dimension_semantics=("arbitrary",),
                                             vmem_limit_bytes=VMEM_LIMIT),
        name=name,
    )(*args)


def _s5_inproj_body(x_ref, mod_ref, w_ref, o_ref, *, tiled):
    sh = mod_ref[0, 0:1, :]
    sc1 = 1.0 + mod_ref[0, 1:2, :]
    h = (x_ref[...] * sc1 + sh).astype(_BF16)
    u = _dot(h, w_ref[...]).astype(_BF16)
    if tiled:
        for j in range(E_SSM // LANES):
            o_ref[j] = u[:, j * LANES:(j + 1) * LANES].reshape(o_ref.shape[1], T_CHUNK, LANES)
    else:
        o_ref[...] = u


def _s5_inproj_call(x2d, mod3, w_u, *, tb, tokens_per_batch, mod_per_batch, tiled, name):
    nt = x2d.shape[0]
    bpb = tokens_per_batch // tb
    mod_map = (lambda i: (i // bpb, 0, 0)) if mod_per_batch else (lambda i: (0, 0, 0))
    if tiled:
        seg_tokens = tokens_per_batch // N_SEG
        bps = seg_tokens // tb
        n_cl = seg_tokens // T_CHUNK
        n_bseg = nt // seg_tokens
        cb = tb // T_CHUNK
        out_shape = jax.ShapeDtypeStruct((E_SSM // LANES, n_cl, n_bseg, T_CHUNK, LANES), _BF16)
        out_spec = pl.BlockSpec((E_SSM // LANES, cb, None, T_CHUNK, LANES),
                                lambda i: (0, i % bps, i // bps, 0, 0))
    else:
        out_shape = jax.ShapeDtypeStruct((nt, E_SSM), _BF16)
        out_spec = pl.BlockSpec((tb, E_SSM), lambda i: (i, 0))
    return pl.pallas_call(
        functools.partial(_s5_inproj_body, tiled=tiled),
        grid=(nt // tb,),
        in_specs=[
            pl.BlockSpec((tb, D_MODEL), lambda i: (i, 0)),
            pl.BlockSpec((1, 3, D_MODEL), mod_map),
            _const_spec((D_MODEL, E_SSM)),
        ],
        out_specs=out_spec,
        out_shape=out_shape,
        compiler_params=pltpu.CompilerParams(dimension_semantics=("arbitrary",),
                                             vmem_limit_bytes=VMEM_LIMIT),
        name=name,
    )(x2d, mod3, w_u)


GROUPS_PER_TILE = 128 // SSM_GROUP
SHUFFLE_ROWS = 512


def _block_transpose8(v, lane):
    for d in (4, 2, 1):
        s = d * SSM_GROUP
        lo = (lane & s) == 0
        nv = list(v)
        for i in range(GROUPS_PER_TILE):
            if i & d == 0:
                a, b = v[i], v[i + d]
                nv[i] = jnp.where(lo, a, pltpu.roll(b, s, 1))
                nv[i + d] = jnp.where(lo, pltpu.roll(a, LANES - s, 1), b)
        v = nv
    return v


def _shuffle_body(x_ref, o_ref, *, to_groups):
    lane = lax.broadcasted_iota(jnp.int32, (8, LANES), 1)

    def chunk(c, carry):
        r0 = pl.multiple_of(c * 16, 16)
        for th in range(T_CHUNK // GROUPS_PER_TILE):
            if to_groups:
                v = [pltpu.bitcast(x_ref[0, pl.ds(r0, 16), pl.ds((th * 8 + a) * LANES, LANES)], jnp.uint32)
                     for a in range(8)]
            else:
                v = [pltpu.bitcast(x_ref[a, pl.ds(r0, 16), pl.ds(th * LANES, LANES)], jnp.uint32)
                     for a in range(8)]
            w = _block_transpose8(v, lane)
            for a in range(8):
                wa = pltpu.bitcast(w[a], _BF16)
                if to_groups:
                    o_ref[a, pl.ds(r0, 16), pl.ds(th * LANES, LANES)] = wa
                else:
                    o_ref[0, pl.ds(r0, 16), pl.ds((th * 8 + a) * LANES, LANES)] = wa
        return carry

    lax.fori_loop(0, SHUFFLE_ROWS // 16, chunk, 0)


def _shuffle_call(x, *, to_groups):
    if to_groups:
        nj, nrow, _ = x.shape
        in_spec = pl.BlockSpec((1, SHUFFLE_ROWS, T_CHUNK * LANES), lambda j, r: (j, r, 0))
        out_spec = pl.BlockSpec((GROUPS_PER_TILE, SHUFFLE_ROWS, TP), lambda j, r: (j, r, 0))
        out_shape = jax.ShapeDtypeStruct((nj * GROUPS_PER_TILE, nrow, TP), _BF16)
    else:
        ng, nrow, _ = x.shape
        nj = ng // GROUPS_PER_TILE
        in_spec = pl.BlockSpec((GROUPS_PER_TILE, SHUFFLE_ROWS, TP), lambda j, r: (j, r, 0))
        out_spec = pl.BlockSpec((1, SHUFFLE_ROWS, T_CHUNK * LANES), lambda j, r: (j, r, 0))
        out_shape = jax.ShapeDtypeStruct((nj, nrow, T_CHUNK * LANES), _BF16)
    return pl.pallas_call(
        functools.partial(_shuffle_body, to_groups=to_groups),
        grid=(nj, nrow // SHUFFLE_ROWS),
        in_specs=[in_spec],
        out_specs=out_spec,
        out_shape=out_shape,
        compiler_params=pltpu.CompilerParams(dimension_semantics=("arbitrary", "arbitrary")),
        name="shuffle_to_groups" if to_groups else "shuffle_to_tiles",
    )(x)


def _cmul_add(ar, ai, hr, hi, sr, si):
    return ar * hr - ai * hi + sr, ar * hi + ai * hr + si


def _s5_core_body(u_ref, uc_ref, m_ref, bm_ref, cm_ref, coef_ref, y_ref,
                  s_ref, h_ref, f_ref, gf_ref, gb_ref, *, n_cl, n_ctx_chunks, n_batch):
    half = SSM_STATE
    rows = 2 * N_SEG
    bm = bm_ref[0]
    s_ref[...] = _dot(u_ref[0], bm)
    sc = _dot(uc_ref[0], bm)
    ar = coef_ref[0, 0:1, :]
    ai = coef_ref[0, 1:2, :]
    pr = coef_ref[0, 2:3, :]
    pi = coef_ref[0, 3:4, :]

    is_fwd8 = lax.broadcasted_iota(jnp.int32, (8, 2 * half), 1) < half
    hr = jnp.zeros((8, 2 * half), _F32)
    hi = jnp.zeros((8, 2 * half), _F32)
    for k in range(n_ctx_chunks):
        kb = n_ctx_chunks - 1 - k
        sr = jnp.where(is_fwd8, sc[k * 8:(k + 1) * 8, 0:2 * half], sc[kb * 8:(kb + 1) * 8, 0:2 * half])
        si = jnp.where(is_fwd8, sc[k * 8:(k + 1) * 8, 2 * half:], sc[kb * 8:(kb + 1) * 8, 2 * half:])
        hr, hi = _cmul_add(ar, ai, hr, hi, sr, si)
    ctx_r, ctx_i = hr, hi

    is_fwd = lax.broadcasted_iota(jnp.int32, (rows, 2 * half), 1) < half

    def load_s(k):
        kf = pl.multiple_of(k * rows, rows)
        kb = pl.multiple_of((n_cl - 1 - k) * rows, rows)
        sf = s_ref[pl.ds(kf, rows), :]
        sb = s_ref[pl.ds(kb, rows), :]
        return (jnp.where(is_fwd, sf[:, 0:2 * half], sb[:, 0:2 * half]),
                jnp.where(is_fwd, sf[:, 2 * half:], sb[:, 2 * half:]))

    def pass1(k, carry):
        sr, si = load_s(k)
        return _cmul_add(ar, ai, carry[0], carry[1], sr, si)

    zero = jnp.zeros((rows, 2 * half), _F32)
    f_r, f_i = lax.fori_loop(0, n_cl, pass1, (zero, zero))
    f_ref[0] = f_r
    f_ref[1] = f_i

    for b in range(n_batch):
        gr = ctx_r[b:b + 1, :]
        gi = ctx_i[b:b + 1, :]
        gf_ref[0, b * N_SEG:b * N_SEG + 1, :] = gr
        gf_ref[1, b * N_SEG:b * N_SEG + 1, :] = gi
        for s in range(1, N_SEG):
            row = b * N_SEG + s
            gr, gi = _cmul_add(pr, pi, gr, gi, f_ref[0, row - 1:row, :], f_ref[1, row - 1:row, :])
            gf_ref[0, row:row + 1, :] = gr
            gf_ref[1, row:row + 1, :] = gi
        gr = ctx_r[b:b + 1, :]
        gi = ctx_i[b:b + 1, :]
        last = b * N_SEG + N_SEG - 1
        gb_ref[0, last:last + 1, :] = gr
        gb_ref[1, last:last + 1, :] = gi
        for s in range(N_SEG - 2, -1, -1):
            row = b * N_SEG + s
            gr, gi = _cmul_add(pr, pi, gr, gi, f_ref[0, row + 1:row + 2, :], f_ref[1, row + 1:row + 2, :])
            gb_ref[0, row:row + 1, :] = gr
            gb_ref[1, row:row + 1, :] = gi

    def pass2(k, carry):
        hr, hi = carry
        kf = pl.multiple_of(k * rows, rows)
        kb = pl.multiple_of((n_cl - 1 - k) * rows, rows)
        h_ref[pl.ds(kf, rows), 0:half] = hr[:, 0:half]
        h_ref[pl.ds(kf, rows), 2 * half:3 * half] = hi[:, 0:half]
        h_ref[pl.ds(kb, rows), half:2 * half] = hr[:, half:]
        h_ref[pl.ds(kb, rows), 3 * half:] = hi[:, half:]
        sr, si = load_s(k)
        return _cmul_add(ar, ai, hr, hi, sr, si)

    h0r = jnp.where(is_fwd, gf_ref[0], gb_ref[0])
    h0i = jnp.where(is_fwd, gf_ref[1], gb_ref[1])
    lax.fori_loop(0, n_cl, pass2, (h0r, h0i))

    y = _dot(u_ref[0], m_ref[0]) + _dot(h_ref[...].astype(_BF16), cm_ref[0])
    y_ref[0] = y.astype(_BF16)


def _s5_core_call(ug, ucg, m_op, b_op, c_op, coef, *, n_cl, n_ctx_chunks, n_batch):
    g, nrow, _ = ug.shape
    ncrow = ucg.shape[1]
    rows = 2 * N_SEG
    body = functools.partial(_s5_core_body, n_cl=n_cl, n_ctx_chunks=n_ctx_chunks, n_batch=n_batch)
    gspec = lambda shape: pl.BlockSpec((1,) + shape, lambda i: (i, 0, 0))
    return pl.pallas_call(
        body,
        grid=(g,),
        in_specs=[gspec((nrow, TP)), gspec((ncrow, TP)), gspec((TP, TP)), gspec((TP, 4 * SSM_STATE)),
                  gspec((4 * SSM_STATE, TP)), gspec((8, 2 * SSM_STATE))],
        out_specs=gspec((nrow, TP)),
        out_shape=jax.ShapeDtypeStruct((g, nrow, TP), _BF16),
        scratch_shapes=[
            pltpu.VMEM((nrow, 4 * SSM_STATE), _F32),
            pltpu.VMEM((nrow, 4 * SSM_STATE), _F32),
            pltpu.VMEM((2, rows, 2 * SSM_STATE), _F32),
            pltpu.VMEM((2, rows, 2 * SSM_STATE), _F32),
            pltpu.VMEM((2, rows, 2 * SSM_STATE), _F32),
        ],
        compiler_params=pltpu.CompilerParams(dimension_semantics=("arbitrary",),
                                             vmem_limit_bytes=VMEM_LIMIT),
        name="s5_core",
    )(ug, ucg, m_op, b_op, c_op, coef)


def _s5_out_body(x_ref, y_ref, mod_ref, wz_ref, wglu_ref, bglu_ref, wout_ref, g_ref, b_ref, o_ref):
    sh = mod_ref[0, 0:1, :]
    sc1 = 1.0 + mod_ref[0, 1:2, :]
    gt = mod_ref[0, 2:3, :]
    x = x_ref[...]
    tb = x.shape[0]
    h = (x * sc1 + sh).astype(_BF16)
    y = jnp.concatenate([y_ref[j].reshape(tb, LANES) for j in range(E_SSM // LANES)], axis=-1)
    g = _gelu_tanh(y.astype(_F32))
    gl = _dot(g.astype(_BF16), wglu_ref[...]) + bglu_ref[...]
    g = g * jax.nn.sigmoid(gl)
    z = _dot(h, wz_ref[...])
    o = _dot((g * _silu(z)).astype(_BF16), wout_ref[...])
    r = DN_ALPHA * x + gt * o
    o_ref[...] = _layernorm_rows(r, g_ref[...], b_ref[...])


def _s5_out_call(x2d, y5d, mod3, w_z, w_glu, b_glu, w_out, ln_g, ln_b, *, tb, tokens_per_batch):
    nt = x2d.shape[0]
    bpb = tokens_per_batch // tb
    bps = tokens_per_batch // N_SEG // tb
    return pl.pallas_call(
        _s5_out_body,
        grid=(nt // tb,),
        in_specs=[
            pl.BlockSpec((tb, D_MODEL), lambda i: (i, 0)),
            pl.BlockSpec((E_SSM // LANES, tb // T_CHUNK, None, T_CHUNK, LANES),
                         lambda i: (0, i % bps, i // bps, 0, 0)),
            pl.BlockSpec((1, 3, D_MODEL), lambda i: (i // bpb, 0, 0)),
            _const_spec((D_MODEL, E_SSM)),
            _const_spec((E_SSM, E_SSM)),
            _const_spec((1, E_SSM)),
            _const_spec((E_SSM, D_MODEL)),
            _const_spec((1, D_MODEL)),
            _const_spec((1, D_MODEL)),
        ],
        out_specs=pl.BlockSpec((tb, D_MODEL), lambda i: (i, 0)),
        out_shape=jax.ShapeDtypeStruct((nt, D_MODEL), _F32),
        compiler_params=pltpu.CompilerParams(dimension_semantics=("arbitrary",),
                                             vmem_limit_bytes=VMEM_LIMIT),
        name="s5_out",
    )(x2d, y5d, mod3, w_z, w_glu, b_glu, w_out, ln_g, ln_b)


def _s5_operators(lam_re, lam_im, log_step, b_re, b_im, c_re, c_im, d, n_cl):
    t = T_CHUNK
    hi = lax.Precision.HIGHEST
    dt = jnp.exp(log_step)[..., None]
    mag = jnp.exp(lam_re * dt)
    a_r = mag * jnp.cos(lam_im * dt)
    a_i = mag * jnp.sin(lam_im * dt)
    q_r, q_i = a_r - 1.0, a_i
    den = lam_re * lam_re + lam_im * lam_im
    f_r = (q_r * lam_re + q_i * lam_im) / den
    f_i = (q_i * lam_re - q_r * lam_im) / den
    bb_r = f_r[..., None] * b_re - f_i[..., None] * b_im
    bb_i = f_r[..., None] * b_im + f_i[..., None] * b_re

    pw_r = [jnp.ones_like(a_r)]
    pw_i = [jnp.zeros_like(a_i)]
    for _ in range(t):
        pr_, pi_ = pw_r[-1], pw_i[-1]
        pw_r.append(pr_ * a_r - pi_ * a_i)
        pw_i.append(pr_ * a_i + pi_ * a_r)
    pw_r = jnp.stack(pw_r)
    pw_i = jnp.stack(pw_i)
    s_r, s_i = pw_r[t], pw_i[t]
    for _ in range(int(round(math.log2(n_cl)))):
        s_r, s_i = s_r * s_r - s_i * s_i, 2.0 * s_r * s_i

    pg_r = pw_r.transpose(1, 2, 0, 3)
    pg_i = pw_i.transpose(1, 2, 0, 3)
    pn_r = pw_r.transpose(1, 2, 3, 0)
    pn_i = pw_i.transpose(1, 2, 3, 0)
    bt_r = bb_r.transpose(0, 1, 3, 2)[:, :, None]
    bt_i = bb_i.transpose(0, 1, 3, 2)[:, :, None]
    ct_r = c_re.transpose(0, 1, 3, 2)[:, :, :, None]
    ct_i = c_im.transpose(0, 1, 3, 2)[:, :, :, None]

    ab_r = pg_r[:, :, :t, None, :] * bt_r - pg_i[:, :, :t, None, :] * bt_i
    ab_i = pg_r[:, :, :t, None, :] * bt_i + pg_i[:, :, :t, None, :] * bt_r
    b_op = jnp.concatenate([ab_r[0][:, ::-1], ab_r[1], ab_i[0][:, ::-1], ab_i[1]], axis=-1)
    b_op = b_op.reshape(N_GROUPS, TP, 4 * SSM_STATE)

    ca_r = ct_r * pn_r[..., 1:, None] - ct_i * pn_i[..., 1:, None]
    ca_i = ct_r * pn_i[..., 1:, None] + ct_i * pn_r[..., 1:, None]
    c_op = jnp.concatenate([ca_r[0], ca_r[1][:, :, ::-1], -ca_i[0], -ca_i[1][:, :, ::-1]], axis=1)
    c_op = c_op.reshape(N_GROUPS, 4 * SSM_STATE, TP)

    abn_r = (pn_r[..., :t, None] * bb_r[:, :, :, None, :] - pn_i[..., :t, None] * bb_i[:, :, :, None, :])
    abn_i = (pn_r[..., :t, None] * bb_i[:, :, :, None, :] + pn_i[..., :t, None] * bb_r[:, :, :, None, :])
    abn_r = abn_r.reshape(2, N_GROUPS, SSM_STATE, TP)
    abn_i = abn_i.reshape(2, N_GROUPS, SSM_STATE, TP)
    kk = (jnp.einsum('rgpn,rgnx->rgpx', c_re, abn_r, precision=hi)
          - jnp.einsum('rgpn,rgnx->rgpx', c_im, abn_i, precision=hi))
    kq = kk.reshape(2, N_GROUPS, SSM_GROUP, t, SSM_GROUP).transpose(0, 1, 4, 3, 2)
    skip = (jnp.eye(SSM_GROUP, dtype=_F32)[None] * d.reshape(N_GROUPS, 1, SSM_GROUP))[:, :, None, :]
    center = kq[0][:, :, 0:1] + kq[1][:, :, 0:1] + skip
    kc = jnp.concatenate([kq[1][:, :, :0:-1], center, kq[0][:, :, 1:]], axis=2)
    m_op = jnp.stack([kc[:, :, t - 1 - s:2 * t - 1 - s] for s in range(t)], axis=1)
    m_op = m_op.reshape(N_GROUPS, TP, TP)

    cat = lambda x: jnp.concatenate([x[0], x[1]], axis=-1)
    coef = jnp.stack([cat(pw_r[t]), cat(pw_i[t]), cat(s_r), cat(s_i)], axis=1)
    coef = jnp.concatenate([coef, jnp.zeros((N_GROUPS, 4, 2 * SSM_STATE), _F32)], axis=1)
    return m_op.astype(_BF16), b_op.astype(_BF16), c_op.astype(_BF16), coef


def kernel(x, c, ctx, c_ctx, ada_w, ada_b, ln_g, ln_b, conv_w_in, conv_w, conv_w_out,
           ssm_w_in, ssm_lam_re, ssm_lam_im, ssm_log_step, ssm_b_re, ssm_b_im,
           ssm_c_re, ssm_c_im, ssm_d, ssm_w_glu, ssm_b_glu, ssm_w_out):
    bsz, seq, _ = x.shape
    ctx_len = ctx.shape[1]
    assert seq % (T_CHUNK * N_SEG) == 0 and ctx_len % T_CHUNK == 0 and bsz == 2
    n_cl = seq // (T_CHUNK * N_SEG)
    n_ctx_chunks = ctx_len // T_CHUNK

    cs = jnp.zeros((8, D_MODEL), _F32).at[:bsz].set(c).at[bsz].set(c_ctx)
    mod = _ada_call(cs, ada_w, ada_b).reshape(DEPTH, 8, 3, D_MODEL)
    mod_x = [mod[i, :bsz] for i in range(DEPTH)]
    mod_c = [mod[i, bsz:bsz + 1] for i in range(DEPTH)]

    x2d = x.reshape(bsz * seq, D_MODEL)
    c2d = ctx.reshape(bsz * ctx_len, D_MODEL)
    g0, b0 = ln_g[0:1], ln_b[0:1]
    g1, b1 = ln_g[1:2], ln_b[1:2]

    w_in0 = conv_w_in[0].astype(_BF16)
    w_out0 = conv_w_out[0].astype(_BF16)
    x1 = _conv_layer_call(x2d, mod_x[0], w_in0, conv_w[0], w_out0, g0, b0, tb=512,
                          tokens_per_batch=seq, period=GRID_W, n_horizontal=E_CONV // 2,
                          halo=True, mod_per_batch=True, name="conv_layer_x")
    c1 = _conv_layer_call(c2d, mod_c[0], w_in0, conv_w[0], w_out0, g0, b0, tb=ctx_len,
                          tokens_per_batch=ctx_len, period=ctx_len, n_horizontal=E_CONV,
                          halo=False, mod_per_batch=False, name="conv_layer_ctx")

    w_in1 = ssm_w_in[0].astype(_BF16)
    w_u, w_z = w_in1[:, :E_SSM], w_in1[:, E_SSM:]
    u_x = _s5_inproj_call(x1, mod_x[1], w_u, tb=1024, tokens_per_batch=seq, mod_per_batch=True,
                          tiled=True, name="s5_inproj_x")
    u_c = _s5_inproj_call(c1, mod_c[1], w_u, tb=ctx_len, tokens_per_batch=ctx_len,
                          mod_per_batch=False, tiled=False, name="s5_inproj_ctx")

    m_op, b_op, c_op, coef = _s5_operators(ssm_lam_re[0], ssm_lam_im[0], ssm_log_step[0],
                                           ssm_b_re[0], ssm_b_im[0], ssm_c_re[0], ssm_c_im[0],
                                           ssm_d[0], n_cl)

    n_rows = n_cl * bsz * N_SEG
    ug = _shuffle_call(u_x.reshape(E_SSM // LANES, n_rows, T_CHUNK * LANES), to_groups=True)
    ucg = u_c.reshape(bsz, n_ctx_chunks, T_CHUNK, N_GROUPS, SSM_GROUP).transpose(3, 1, 0, 2, 4)
    ucg = jnp.pad(ucg, ((0, 0), (0, 0), (0, 8 - bsz), (0, 0), (0, 0)))
    ucg = ucg.reshape(N_GROUPS, n_ctx_chunks * 8, TP)

    yg = _s5_core_call(ug, ucg, m_op, b_op, c_op, coef, n_cl=n_cl, n_ctx_chunks=n_ctx_chunks,
                       n_batch=bsz)
    y = _shuffle_call(yg, to_groups=False)
    y = y.reshape(E_SSM // LANES, n_cl, bsz * N_SEG, T_CHUNK, LANES)

    out = _s5_out_call(x1, y, mod_x[1], w_z, ssm_w_glu[0].astype(_BF16), ssm_b_glu[0:1],
                       ssm_w_out[0].astype(_BF16), g1, b1, tb=512, tokens_per_batch=seq)
    return out.reshape(bsz, seq, D_MODEL)
```

```python
import functools
import math

import jax
import jax.numpy as jnp
from jax import lax
from jax.experimental import pallas as pl
from jax.experimental.pallas import tpu as pltpu

D_MODEL = 1024
GRID_W = 64
E_CONV = 2048
E_SSM = 2048
SSM_GROUP = 16
N_GROUPS = E_SSM // SSM_GROUP
SSM_STATE = 64
LN_EPS = 1e-5
DEPTH = 2
DN_ALPHA = (2 * DEPTH) ** 0.25

LANES = 128
SUBLANES = 8
T_CHUNK = 16
TP = T_CHUNK * SSM_GROUP
GROUPS_PER_TILE = LANES // SSM_GROUP
N_TILES = E_SSM // LANES
T_PITCH = 24
VMEM_LIMIT = 56 * 1024 * 1024

_F32 = jnp.float32
_BF16 = jnp.bfloat16


def _dot(a, b):
    return jnp.dot(a, b, preferred_element_type=_F32)


def _silu(z):
    return z * jax.nn.sigmoid(z)


def _gelu_tanh(x):
    c = math.sqrt(2.0 / math.pi)
    return x * (0.5 * (1.0 + jnp.tanh(c * (x + 0.044715 * (x * x * x)))))


def _layernorm_rows(r, g, b):
    mu = jnp.mean(r, axis=-1, keepdims=True)
    d = r - mu
    var = jnp.mean(d * d, axis=-1, keepdims=True)
    return d * lax.rsqrt(var + LN_EPS) * g + b


def _const_spec(shape):
    nd = len(shape)
    return pl.BlockSpec(shape, lambda *_: (0,) * nd, pipeline_mode=pl.Buffered(1))


def _ada_body(c_ref, w_ref, b_ref, o_ref):
    o_ref[0] = jnp.dot(_silu(c_ref[...]), w_ref[0], precision=lax.Precision.HIGHEST,
                       preferred_element_type=_F32) + b_ref[0]


def _ada_call(cs, ada_w, ada_b):
    depth = ada_w.shape[0]
    rows = cs.shape[0]
    return pl.pallas_call(
        _ada_body,
        grid=(depth, 3),
        in_specs=[
            pl.BlockSpec((rows, D_MODEL), lambda i, j: (0, 0)),
            pl.BlockSpec((1, D_MODEL, D_MODEL), lambda i, j: (i, 0, j)),
            pl.BlockSpec((1, 1, D_MODEL), lambda i, j: (i, 0, j)),
        ],
        out_specs=pl.BlockSpec((1, rows, D_MODEL), lambda i, j: (i, 0, j)),
        out_shape=jax.ShapeDtypeStruct((depth, rows, 3 * D_MODEL), _F32),
        compiler_params=pltpu.CompilerParams(dimension_semantics=("arbitrary", "arbitrary")),
        name="ada_mod",
    )(cs, ada_w, ada_b.reshape(depth, 1, 3 * D_MODEL))


CONV_CK = 512


def _conv_layer_body(*refs, tb, period, n_horizontal, blocks_per_batch, halo):
    if halo:
        x_ref, xp_ref, xn_ref, mod_ref, win_ref, cw_ref, wout_ref, g_ref, b_ref, o_ref = refs
    else:
        x_ref, mod_ref, win_ref, cw_ref, wout_ref, g_ref, b_ref, o_ref = refs
    sh = mod_ref[0, 0:1, :]
    sc1 = 1.0 + mod_ref[0, 1:2, :]
    gt = mod_ref[0, 2:3, :]
    x = x_ref[...]
    h = (x * sc1 + sh).astype(_BF16)
    if halo:
        i = pl.program_id(0)
        jb = i % blocks_per_batch
        prev_ok = (jb != 0).astype(_F32)
        next_ok = (jb != blocks_per_batch - 1).astype(_F32)
        hp = (xp_ref[...] * sc1 + sh).astype(_BF16)
        hn = (xn_ref[...] * sc1 + sh).astype(_BF16)
    pos = lax.broadcasted_iota(jnp.int32, (tb, CONV_CK), 0) % period
    acc = jnp.zeros((tb, D_MODEL), _F32)
    for k in range(E_CONV // CONV_CK):
        c0 = k * CONV_CK
        bg = _dot(h, win_ref[:, c0:c0 + CONV_CK])
        u = _dot(h, win_ref[:, E_CONV + c0:E_CONV + c0 + CONV_CK]) * \
            _dot(h, win_ref[:, 2 * E_CONV + c0:2 * E_CONV + c0 + CONV_CK])
        z = _dot(h, win_ref[:, 3 * E_CONV + c0:3 * E_CONV + c0 + CONV_CK])
        w0 = cw_ref[0:1, c0:c0 + CONV_CK]
        w1 = cw_ref[1:2, c0:c0 + CONV_CK]
        w2 = cw_ref[2:3, c0:c0 + CONV_CK]
        if c0 < n_horizontal:
            um = jnp.where(pos != 0, pltpu.roll(u, 1, 0), 0.0)
            up = jnp.where(pos != period - 1, pltpu.roll(u, tb - 1, 0), 0.0)
        else:
            uph = _dot(hp, win_ref[:, E_CONV + c0:E_CONV + c0 + CONV_CK]) * \
                _dot(hp, win_ref[:, 2 * E_CONV + c0:2 * E_CONV + c0 + CONV_CK]) * prev_ok
            unh = _dot(hn, win_ref[:, E_CONV + c0:E_CONV + c0 + CONV_CK]) * \
                _dot(hn, win_ref[:, 2 * E_CONV + c0:2 * E_CONV + c0 + CONV_CK]) * next_ok
            um = jnp.concatenate([uph, u[:tb - GRID_W]], axis=0)
            up = jnp.concatenate([u[GRID_W:], unh], axis=0)
        yc = um * w0 + u * w1 + up * w2
        gated = (bg * yc * _silu(z)).astype(_BF16)
        acc = acc + _dot(gated, wout_ref[c0:c0 + CONV_CK, :])
    r = DN_ALPHA * x + gt * acc
    o_ref[...] = _layernorm_rows(r, g_ref[...], b_ref[...])


def _conv_layer_call(x2d, mod3, w_in, conv_w, w_out, ln_g, ln_b, *, tb, tokens_per_batch,
                     period, n_horizontal, halo, mod_per_batch, name):
    nt = x2d.shape[0]
    bpb = tokens_per_batch // tb
    hb = tb // GRID_W
    n_hblocks = nt // GRID_W
    mod_map = (lambda i: (i // bpb, 0, 0)) if mod_per_batch else (lambda i: (0, 0, 0))
    in_specs = [pl.BlockSpec((tb, D_MODEL), lambda i: (i, 0))]
    args = [x2d]
    if halo:
        in_specs += [
            pl.BlockSpec((GRID_W, D_MODEL), lambda i: (jnp.maximum(i * hb - 1, 0), 0)),
            pl.BlockSpec((GRID_W, D_MODEL), lambda i: (jnp.minimum((i + 1) * hb, n_hblocks - 1), 0)),
        ]
        args += [x2d, x2d]
    in_specs += [
        pl.BlockSpec((1, 3, D_MODEL), mod_map),
        _const_spec((D_MODEL, 4 * E_CONV)),
        _const_spec((3, E_CONV)),
        _const_spec((E_CONV, D_MODEL)),
        _const_spec((1, D_MODEL)),
        _const_spec((1, D_MODEL)),
    ]
    args += [mod3, w_in, conv_w, w_out, ln_g, ln_b]
    body = functools.partial(_conv_layer_body, tb=tb, period=period, n_horizontal=n_horizontal,
                             blocks_per_batch=bpb, halo=halo)
    return pl.pallas_call(
        body,
        grid=(nt // tb,),
        in_specs=in_specs,
        out_specs=pl.BlockSpec((tb, D_MODEL), lambda i: (i, 0)),
        out_shape=jax.ShapeDtypeStruct((nt, D_MODEL), _F32),
        compiler_params=pltpu.CompilerParams(dimension_semantics=("arbitrary",),
                                             vmem_limit_bytes=VMEM_LIMIT),
        name=name,
    )(*args)


def _block_transpose8(v, lane):
    for d in (4, 2, 1):
        s = d * SSM_GROUP
        lo = (lane & s) == 0
        nv = list(v)
        for i in range(GROUPS_PER_TILE):
            if i & d == 0:
                a, b = v[i], v[i + d]
                nv[i] = jnp.where(lo, a, pltpu.roll(b, s, 1))
                nv[i + d] = jnp.where(lo, pltpu.roll(a, LANES - s, 1), b)
        v = nv
    return v


INPROJ_NC = 256


def _s5_inproj_body(x_ref, mod_ref, w_ref, o_ref, t_ref, *, tb):
    cb = tb // T_CHUNK
    sh = mod_ref[0, 0:1, :]
    sc1 = 1.0 + mod_ref[0, 1:2, :]
    h = (x_ref[...] * sc1 + sh).astype(_BF16)
    lane = lax.broadcasted_iota(jnp.int32, (cb, LANES), 1)
    for c in range(E_SSM // INPROJ_NC):
        u = _dot(h, w_ref[:, c * INPROJ_NC:(c + 1) * INPROJ_NC])
        for half in range(INPROJ_NC // LANES):
            j = c * (INPROJ_NC // LANES) + half
            uj = u[:, half * LANES:(half + 1) * LANES]
            for ch in range(cb):
                t_ref[j, ch * T_PITCH:ch * T_PITCH + T_CHUNK, :] = uj[ch * T_CHUNK:(ch + 1) * T_CHUNK]
            r = [t_ref[j, pl.ds(t, cb, stride=T_PITCH), :] for t in range(T_CHUNK)]
            for th in range(T_CHUNK // GROUPS_PER_TILE):
                w = _block_transpose8(r[th * GROUPS_PER_TILE:(th + 1) * GROUPS_PER_TILE], lane)
                for gg in range(GROUPS_PER_TILE):
                    o_ref[j * GROUPS_PER_TILE + gg, :, th * LANES:(th + 1) * LANES] = w[gg].astype(_BF16)


def _s5_inproj_call(x2d, mod3, w_u, *, tb, tokens_per_batch, mod_per_batch, name):
    nt = x2d.shape[0]
    bpb = tokens_per_batch // tb
    cb = tb // T_CHUNK
    mod_map = (lambda i: (i // bpb, 0, 0)) if mod_per_batch else (lambda i: (0, 0, 0))
    return pl.pallas_call(
        functools.partial(_s5_inproj_body, tb=tb),
        grid=(nt // tb,),
        in_specs=[
            pl.BlockSpec((tb, D_MODEL), lambda i: (i, 0)),
            pl.BlockSpec((1, 3, D_MODEL), mod_map),
            _const_spec((D_MODEL, E_SSM)),
        ],
        out_specs=pl.BlockSpec((N_GROUPS, cb, TP), lambda i: (0, i, 0)),
        out_shape=jax.ShapeDtypeStruct((N_GROUPS, nt // T_CHUNK, TP), _BF16),
        scratch_shapes=[pltpu.VMEM((N_TILES, cb * T_PITCH, LANES), _F32)],
        compiler_params=pltpu.CompilerParams(dimension_semantics=("arbitrary",),
                                             vmem_limit_bytes=VMEM_LIMIT),
        name=name,
    )(x2d, mod3, w_u)


def _cmul_add(ar, ai, hr, hi, sr, si):
    return ar * hr - ai * hi + sr, ar * hi + ai * hr + si


def _s5_core_body(u_ref, uc_ref, m_ref, bm_ref, cm_ref, coef_ref, y_ref,
                  s_ref, sc_ref, h_ref, *, n_chunks, n_ctx_chunks):
    half = SSM_STATE
    g8 = GROUPS_PER_TILE
    for gg in range(g8):
        s = _dot(u_ref[gg], bm_ref[gg])
        s_ref[0, pl.ds(gg, n_chunks, stride=g8), :] = s[:, 0:LANES]
        s_ref[1, pl.ds(gg, n_chunks, stride=g8), :] = s[:, LANES:]
        sc = _dot(uc_ref[gg, 0], bm_ref[gg])
        sc_ref[0, pl.ds(gg, n_ctx_chunks, stride=g8), :] = sc[:, 0:LANES]
        sc_ref[1, pl.ds(gg, n_ctx_chunks, stride=g8), :] = sc[:, LANES:]
    ar = coef_ref[0, 0]
    ai = coef_ref[0, 1]
    is_fwd = lax.broadcasted_iota(jnp.int32, (g8, LANES), 1) < half

    def step(src_ref, kf, kb, hr, hi):
        sr = jnp.where(is_fwd, src_ref[0, pl.ds(kf, g8), :], src_ref[0, pl.ds(kb, g8), :])
        si = jnp.where(is_fwd, src_ref[1, pl.ds(kf, g8), :], src_ref[1, pl.ds(kb, g8), :])
        return _cmul_add(ar, ai, hr, hi, sr, si)

    hr = jnp.zeros((g8, LANES), _F32)
    hi = jnp.zeros((g8, LANES), _F32)
    for k in range(n_ctx_chunks):
        hr, hi = step(sc_ref, k * g8, (n_ctx_chunks - 1 - k) * g8, hr, hi)

    def body(k, carry):
        hr, hi = carry
        kf = pl.multiple_of(k * g8, g8)
        kb = pl.multiple_of((n_chunks - 1 - k) * g8, g8)
        h_ref[0, pl.ds(kf, g8), 0:half] = hr[:, 0:half]
        h_ref[1, pl.ds(kf, g8), 0:half] = hi[:, 0:half]
        h_ref[0, pl.ds(kb, g8), half:] = hr[:, half:]
        h_ref[1, pl.ds(kb, g8), half:] = hi[:, half:]
        return step(s_ref, kf, kb, hr, hi)

    lax.fori_loop(0, n_chunks, body, (hr, hi), unroll=4)

    for gg in range(g8):
        hg = jnp.concatenate([h_ref[0, pl.ds(gg, n_chunks, stride=g8), :],
                              h_ref[1, pl.ds(gg, n_chunks, stride=g8), :]], axis=-1)
        y = _dot(u_ref[gg], m_ref[gg]) + _dot(hg.astype(_BF16), cm_ref[gg])
        y_ref[gg] = y.astype(_BF16)


def _s5_core_call(ug, ucg, m_op, b_op, c_op, coef, *, n_batch):
    g, nrow, _ = ug.shape
    n_chunks = nrow // n_batch
    n_ctx_chunks = ucg.shape[2]
    g8 = GROUPS_PER_TILE
    body = functools.partial(_s5_core_body, n_chunks=n_chunks, n_ctx_chunks=n_ctx_chunks)
    op_spec = pl.BlockSpec((g8, TP, TP), lambda b, j: (j, 0, 0))
    return pl.pallas_call(
        body,
        grid=(n_batch, g // g8),
        in_specs=[
            pl.BlockSpec((g8, n_chunks, TP), lambda b, j: (j, b, 0)),
            pl.BlockSpec((g8, 1, n_ctx_chunks, TP), lambda b, j: (j, b, 0, 0)),
            op_spec, op_spec, op_spec,
            pl.BlockSpec((1, 2, g8, LANES), lambda b, j: (j, 0, 0, 0)),
        ],
        out_specs=pl.BlockSpec((g8, n_chunks, TP), lambda b, j: (j, b, 0)),
        out_shape=jax.ShapeDtypeStruct((g, nrow, TP), _BF16),
        scratch_shapes=[
            pltpu.VMEM((2, n_chunks * g8, LANES), _F32),
            pltpu.VMEM((2, n_ctx_chunks * g8, LANES), _F32),
            pltpu.VMEM((2, n_chunks * g8, LANES), _F32),
        ],
        compiler_params=pltpu.CompilerParams(dimension_semantics=("arbitrary", "arbitrary"),
                                             vmem_limit_bytes=VMEM_LIMIT),
        name="s5_core",
    )(ug, ucg, m_op, b_op, c_op, coef)


def _s5_out_body(x_ref, y_ref, mod_ref, wz_ref, wglu_ref, bglu_ref, wout_ref, g_ref, b_ref, o_ref,
                 t_ref, *, tb):
    cb = tb // T_CHUNK
    sh = mod_ref[0, 0:1, :]
    sc1 = 1.0 + mod_ref[0, 1:2, :]
    gt = mod_ref[0, 2:3, :]
    x = x_ref[...]
    h = (x * sc1 + sh).astype(_BF16)
    lane = lax.broadcasted_iota(jnp.int32, (cb, LANES), 1)
    cols = []
    for j in range(N_TILES):
        for th in range(T_CHUNK // GROUPS_PER_TILE):
            w = [_gelu_tanh(y_ref[j * GROUPS_PER_TILE + gg, :, th * LANES:(th + 1) * LANES].astype(_F32))
                 for gg in range(GROUPS_PER_TILE)]
            r = _block_transpose8(w, lane)
            for a in range(GROUPS_PER_TILE):
                t_ref[j, pl.ds(th * GROUPS_PER_TILE + a, cb, stride=T_PITCH), :] = r[a]
        cols.append(jnp.concatenate(
            [t_ref[j, ch * T_PITCH:ch * T_PITCH + T_CHUNK, :] for ch in range(cb)], axis=0))
    g = jnp.concatenate(cols, axis=-1)
    gl = _dot(g.astype(_BF16), wglu_ref[...]) + bglu_ref[...]
    g = g * jax.nn.sigmoid(gl)
    z = _dot(h, wz_ref[...])
    o = _dot((g * _silu(z)).astype(_BF16), wout_ref[...])
    r = DN_ALPHA * x + gt * o
    o_ref[...] = _layernorm_rows(r, g_ref[...], b_ref[...])


def _s5_out_call(x2d, yg, mod3, w_z, w_glu, b_glu, w_out, ln_g, ln_b, *, tb, tokens_per_batch):
    nt = x2d.shape[0]
    bpb = tokens_per_batch // tb
    cb = tb // T_CHUNK
    return pl.pallas_call(
        functools.partial(_s5_out_body, tb=tb),
        grid=(nt // tb,),
        in_specs=[
            pl.BlockSpec((tb, D_MODEL), lambda i: (i, 0)),
            pl.BlockSpec((N_GROUPS, cb, TP), lambda i: (0, i, 0)),
            pl.BlockSpec((1, 3, D_MODEL), lambda i: (i // bpb, 0, 0)),
            _const_spec((D_MODEL, E_SSM)),
            _const_spec((E_SSM, E_SSM)),
            _const_spec((1, E_SSM)),
            _const_spec((E_SSM, D_MODEL)),
            _const_spec((1, D_MODEL)),
            _const_spec((1, D_MODEL)),
        ],
        out_specs=pl.BlockSpec((tb, D_MODEL), lambda i: (i, 0)),
        out_shape=jax.ShapeDtypeStruct((nt, D_MODEL), _F32),
        scratch_shapes=[pltpu.VMEM((N_TILES, cb * T_PITCH, LANES), _F32)],
        compiler_params=pltpu.CompilerParams(dimension_semantics=("arbitrary",),
                                             vmem_limit_bytes=VMEM_LIMIT),
        name="s5_out",
    )(x2d, yg, mod3, w_z, w_glu, b_glu, w_out, ln_g, ln_b)


def _s5_operators(lam_re, lam_im, log_step, b_re, b_im, c_re, c_im, d):
    t = T_CHUNK
    hi = lax.Precision.HIGHEST
    dt = jnp.exp(log_step)[..., None]
    mag = jnp.exp(lam_re * dt)
    a_r = mag * jnp.cos(lam_im * dt)
    a_i = mag * jnp.sin(lam_im * dt)
    q_r, q_i = a_r - 1.0, a_i
    den = lam_re * lam_re + lam_im * lam_im
    f_r = (q_r * lam_re + q_i * lam_im) / den
    f_i = (q_i * lam_re - q_r * lam_im) / den
    bb_r = f_r[..., None] * b_re - f_i[..., None] * b_im
    bb_i = f_r[..., None] * b_im + f_i[..., None] * b_re

    pw_r = [jnp.ones_like(a_r)]
    pw_i = [jnp.zeros_like(a_i)]
    for _ in range(t):
        pr_, pi_ = pw_r[-1], pw_i[-1]
        pw_r.append(pr_ * a_r - pi_ * a_i)
        pw_i.append(pr_ * a_i + pi_ * a_r)
    pw_r = jnp.stack(pw_r)
    pw_i = jnp.stack(pw_i)

    pg_r = pw_r.transpose(1, 2, 0, 3)
    pg_i = pw_i.transpose(1, 2, 0, 3)
    pn_r = pw_r.transpose(1, 2, 3, 0)
    pn_i = pw_i.transpose(1, 2, 3, 0)
    bt_r = bb_r.transpose(0, 1, 3, 2)[:, :, None]
    bt_i = bb_i.transpose(0, 1, 3, 2)[:, :, None]
    ct_r = c_re.transpose(0, 1, 3, 2)[:, :, :, None]
    ct_i = c_im.transpose(0, 1, 3, 2)[:, :, :, None]

    ab_r = pg_r[:, :, :t, None, :] * bt_r - pg_i[:, :, :t, None, :] * bt_i
    ab_i = pg_r[:, :, :t, None, :] * bt_i + pg_i[:, :, :t, None, :] * bt_r
    b_op = jnp.concatenate([ab_r[0][:, ::-1], ab_r[1], ab_i[0][:, ::-1], ab_i[1]], axis=-1)
    b_op = b_op.reshape(N_GROUPS, TP, 4 * SSM_STATE)

    ca_r = ct_r * pn_r[..., 1:, None] - ct_i * pn_i[..., 1:, None]
    ca_i = ct_r * pn_i[..., 1:, None] + ct_i * pn_r[..., 1:, None]
    c_op = jnp.concatenate([ca_r[0], ca_r[1][:, :, ::-1], -ca_i[0], -ca_i[1][:, :, ::-1]], axis=1)
    c_op = c_op.reshape(N_GROUPS, 4 * SSM_STATE, TP)

    abn_r = (pn_r[..., :t, None] * bb_r[:, :, :, None, :] - pn_i[..., :t, None] * bb_i[:, :, :, None, :])
    abn_i = (pn_r[..., :t, None] * bb_i[:, :, :, None, :] + pn_i[..., :t, None] * bb_r[:, :, :, None, :])
    abn_r = abn_r.reshape(2, N_GROUPS, SSM_STATE, TP)
    abn_i = abn_i.reshape(2, N_GROUPS, SSM_STATE, TP)
    kk = (jnp.einsum('rgpn,rgnx->rgpx', c_re, abn_r, precision=hi)
          - jnp.einsum('rgpn,rgnx->rgpx', c_im, abn_i, precision=hi))
    kq = kk.reshape(2, N_GROUPS, SSM_GROUP, t, SSM_GROUP).transpose(0, 1, 4, 3, 2)
    skip = (jnp.eye(SSM_GROUP, dtype=_F32)[None] * d.reshape(N_GROUPS, 1, SSM_GROUP))[:, :, None, :]
    center = kq[0][:, :, 0:1] + kq[1][:, :, 0:1] + skip
    kc = jnp.concatenate([kq[1][:, :, :0:-1], center, kq[0][:, :, 1:]], axis=2)
    m_op = jnp.stack([kc[:, :, t - 1 - s:2 * t - 1 - s] for s in range(t)], axis=1)
    m_op = m_op.reshape(N_GROUPS, TP, TP)

    cat = lambda x: jnp.concatenate([x[0], x[1]], axis=-1)
    coef = jnp.stack([cat(pw_r[t]), cat(pw_i[t])], axis=1)
    coef = coef.reshape(N_TILES, GROUPS_PER_TILE, 2, 2 * SSM_STATE).transpose(0, 2, 1, 3)
    return m_op.astype(_BF16), b_op.astype(_BF16), c_op.astype(_BF16), coef


def kernel(x, c, ctx, c_ctx, ada_w, ada_b, ln_g, ln_b, conv_w_in, conv_w, conv_w_out,
           ssm_w_in, ssm_lam_re, ssm_lam_im, ssm_log_step, ssm_b_re, ssm_b_im,
           ssm_c_re, ssm_c_im, ssm_d, ssm_w_glu, ssm_b_glu, ssm_w_out):
    bsz, seq, _ = x.shape
    ctx_len = ctx.shape[1]
    assert seq % 1024 == 0 and ctx_len % T_CHUNK == 0 and bsz < 8

    cs = jnp.zeros((8, D_MODEL), _F32).at[:bsz].set(c).at[bsz].set(c_ctx)
    mod = _ada_call(cs, ada_w, ada_b).reshape(DEPTH, 8, 3, D_MODEL)
    mod_x = [mod[i, :bsz] for i in range(DEPTH)]
    mod_c = [mod[i, bsz:bsz + 1] for i in range(DEPTH)]

    x2d = x.reshape(bsz * seq, D_MODEL)
    c2d = ctx.reshape(bsz * ctx_len, D_MODEL)
    g0, b0 = ln_g[0:1], ln_b[0:1]
    g1, b1 = ln_g[1:2], ln_b[1:2]

    w_in0 = conv_w_in[0].astype(_BF16)
    w_out0 = conv_w_out[0].astype(_BF16)
    x1 = _conv_layer_call(x2d, mod_x[0], w_in0, conv_w[0], w_out0, g0, b0, tb=512,
                          tokens_per_batch=seq, period=GRID_W, n_horizontal=E_CONV // 2,
                          halo=True, mod_per_batch=True, name="conv_layer_x")
    c1 = _conv_layer_call(c2d, mod_c[0], w_in0, conv_w[0], w_out0, g0, b0, tb=ctx_len,
                          tokens_per_batch=ctx_len, period=ctx_len, n_horizontal=E_CONV,
                          halo=False, mod_per_batch=False, name="conv_layer_ctx")

    w_in1 = ssm_w_in[0].astype(_BF16)
    w_u, w_z = w_in1[:, :E_SSM], w_in1[:, E_SSM:]
    ug = _s5_inproj_call(x1, mod_x[1], w_u, tb=512, tokens_per_batch=seq, mod_per_batch=True,
                         name="s5_inproj_x")
    ucg = _s5_inproj_call(c1, mod_c[1], w_u, tb=ctx_len, tokens_per_batch=ctx_len,
                          mod_per_batch=False, name="s5_inproj_ctx")
    ucg = ucg.reshape(N_GROUPS, bsz, ctx_len // T_CHUNK, TP)

    m_op, b_op, c_op, coef = _s5_operators(ssm_lam_re[0], ssm_lam_im[0], ssm_log_step[0],
                                           ssm_b_re[0], ssm_b_im[0], ssm_c_re[0], ssm_c_im[0],
                                           ssm_d[0])
    yg = _s5_core_call(ug, ucg, m_op, b_op, c_op, coef, n_batch=bsz)

    out = _s5_out_call(x1, yg, mod_x[1], w_z, ssm_w_glu[0].astype(_BF16), ssm_b_glu[0:1],
                       ssm_w_out[0].astype(_BF16), g1, b1, tb=512, tokens_per_batch=seq)
    return out.reshape(bsz, seq, D_MODEL)
```

```python
import functools
import math

import jax
import jax.numpy as jnp
from jax import lax
from jax.experimental import pallas as pl
from jax.experimental.pallas import tpu as pltpu

D_MODEL = 1024
GRID_W = 64
E_CONV = 2048
E_SSM = 2048
SSM_GROUP = 16
N_GROUPS = E_SSM // SSM_GROUP
SSM_STATE = 64
LN_EPS = 1e-5
DEPTH = 2
DN_ALPHA = (2 * DEPTH) ** 0.25

LANES = 128
SUBLANES = 8
T_CHUNK = 16
TP = T_CHUNK * SSM_GROUP
GROUPS_PER_TILE = LANES // SSM_GROUP
N_TILES = E_SSM // LANES
T_PITCH = 24
VMEM_LIMIT = 56 * 1024 * 1024

_F32 = jnp.float32
_BF16 = jnp.bfloat16


def _dot(a, b):
    return jnp.dot(a, b, preferred_element_type=_F32)


def _silu(z):
    return z * jax.nn.sigmoid(z)


def _gelu_tanh(x):
    c = math.sqrt(2.0 / math.pi)
    return x * (0.5 * (1.0 + jnp.tanh(c * (x + 0.044715 * (x * x * x)))))


def _layernorm_rows(r, g, b):
    mu = jnp.mean(r, axis=-1, keepdims=True)
    d = r - mu
    var = jnp.mean(d * d, axis=-1, keepdims=True)
    return d * lax.rsqrt(var + LN_EPS) * g + b


def _const_spec(shape):
    nd = len(shape)
    return pl.BlockSpec(shape, lambda *_: (0,) * nd, pipeline_mode=pl.Buffered(1))


def _ada_body(c_ref, w_ref, b_ref, o_ref):
    o_ref[0] = jnp.dot(_silu(c_ref[...]), w_ref[0], precision=lax.Precision.HIGHEST,
                       preferred_element_type=_F32) + b_ref[0]


def _ada_call(cs, ada_w, ada_b):
    depth = ada_w.shape[0]
    rows = cs.shape[0]
    return pl.pallas_call(
        _ada_body,
        grid=(depth, 3),
        in_specs=[
            pl.BlockSpec((rows, D_MODEL), lambda i, j: (0, 0)),
            pl.BlockSpec((1, D_MODEL, D_MODEL), lambda i, j: (i, 0, j)),
            pl.BlockSpec((1, 1, D_MODEL), lambda i, j: (i, 0, j)),
        ],
        out_specs=pl.BlockSpec((1, rows, D_MODEL), lambda i, j: (i, 0, j)),
        out_shape=jax.ShapeDtypeStruct((depth, rows, 3 * D_MODEL), _F32),
        compiler_params=pltpu.CompilerParams(dimension_semantics=("arbitrary", "arbitrary")),
        name="ada_mod",
    )(cs, ada_w, ada_b.reshape(depth, 1, 3 * D_MODEL))


CONV_CK = 256


def _conv_layer_body(*refs, tb, period, n_horizontal, blocks_per_batch, halo):
    if halo:
        x_ref, xp_ref, xn_ref, mod_ref, win_ref, cw_ref, wout_ref, g_ref, b_ref, o_ref = refs
    else:
        x_ref, mod_ref, win_ref, cw_ref, wout_ref, g_ref, b_ref, o_ref = refs
    sh = mod_ref[0, 0:1, :]
    sc1 = 1.0 + mod_ref[0, 1:2, :]
    gt = mod_ref[0, 2:3, :]
    x = x_ref[...]
    h = (x * sc1 + sh).astype(_BF16)
    if halo:
        i = pl.program_id(0)
        jb = i % blocks_per_batch
        prev_ok = (jb != 0).astype(_F32)
        next_ok = (jb != blocks_per_batch - 1).astype(_F32)
        hp = (xp_ref[...] * sc1 + sh).astype(_BF16)
        hn = (xn_ref[...] * sc1 + sh).astype(_BF16)
    pos = lax.broadcasted_iota(jnp.int32, (tb, CONV_CK), 0) % period
    acc = jnp.zeros((tb, D_MODEL), _F32)
    for k in range(E_CONV // CONV_CK):
        c0 = k * CONV_CK
        bg = _dot(h, win_ref[:, c0:c0 + CONV_CK])
        u = _dot(h, win_ref[:, E_CONV + c0:E_CONV + c0 + CONV_CK]) * \
            _dot(h, win_ref[:, 2 * E_CONV + c0:2 * E_CONV + c0 + CONV_CK])
        z = _dot(h, win_ref[:, 3 * E_CONV + c0:3 * E_CONV + c0 + CONV_CK])
        w0 = cw_ref[0:1, c0:c0 + CONV_CK]
        w1 = cw_ref[1:2, c0:c0 + CONV_CK]
        w2 = cw_ref[2:3, c0:c0 + CONV_CK]
        if c0 < n_horizontal:
            um = jnp.where(pos != 0, pltpu.roll(u, 1, 0), 0.0)
            up = jnp.where(pos != period - 1, pltpu.roll(u, tb - 1, 0), 0.0)
        else:
            uph = _dot(hp, win_ref[:, E_CONV + c0:E_CONV + c0 + CONV_CK]) * \
                _dot(hp, win_ref[:, 2 * E_CONV + c0:2 * E_CONV + c0 + CONV_CK]) * prev_ok
            unh = _dot(hn, win_ref[:, E_CONV + c0:E_CONV + c0 + CONV_CK]) * \
                _dot(hn, win_ref[:, 2 * E_CONV + c0:2 * E_CONV + c0 + CONV_CK]) * next_ok
            um = jnp.concatenate([uph, u[:tb - GRID_W]], axis=0)
            up = jnp.concatenate([u[GRID_W:], unh], axis=0)
        yc = um * w0 + u * w1 + up * w2
        gated = (bg * yc * _silu(z)).astype(_BF16)
        acc = acc + _dot(gated, wout_ref[c0:c0 + CONV_CK, :])
    r = DN_ALPHA * x + gt * acc
    o_ref[...] = _layernorm_rows(r, g_ref[...], b_ref[...])


def _conv_layer_call(x2d, mod3, w_in, conv_w, w_out, ln_g, ln_b, *, tb, tokens_per_batch,
                     period, n_horizontal, halo, mod_per_batch, name):
    nt = x2d.shape[0]
    bpb = tokens_per_batch // tb
    hb = tb // GRID_W
    n_hblocks = nt // GRID_W
    mod_map = (lambda i: (i // bpb, 0, 0)) if mod_per_batch else (lambda i: (0, 0, 0))
    in_specs = [pl.BlockSpec((tb, D_MODEL), lambda i: (i, 0))]
    args = [x2d]
    if halo:
        in_specs += [
            pl.BlockSpec((GRID_W, D_MODEL), lambda i: (jnp.maximum(i * hb - 1, 0), 0)),
            pl.BlockSpec((GRID_W, D_MODEL), lambda i: (jnp.minimum((i + 1) * hb, n_hblocks - 1), 0)),
        ]
        args += [x2d, x2d]
    in_specs += [
        pl.BlockSpec((1, 3, D_MODEL), mod_map),
        _const_spec((D_MODEL, 4 * E_CONV)),
        _const_spec((3, E_CONV)),
        _const_spec((E_CONV, D_MODEL)),
        _const_spec((1, D_MODEL)),
        _const_spec((1, D_MODEL)),
    ]
    args += [mod3, w_in, conv_w, w_out, ln_g, ln_b]
    body = functools.partial(_conv_layer_body, tb=tb, period=period, n_horizontal=n_horizontal,
                             blocks_per_batch=bpb, halo=halo)
    return pl.pallas_call(
        body,
        grid=(nt // tb,),
        in_specs=in_specs,
        out_specs=pl.BlockSpec((tb, D_MODEL), lambda i: (i, 0)),
        out_shape=jax.ShapeDtypeStruct((nt, D_MODEL), _F32),
        compiler_params=pltpu.CompilerParams(dimension_semantics=("arbitrary",),
                                             vmem_limit_bytes=VMEM_LIMIT),
        name=name,
    )(*args)


def _block_transpose8(v, lane):
    for d in (4, 2, 1):
        s = d * SSM_GROUP
        lo = (lane & s) == 0
        nv = list(v)
        for i in range(GROUPS_PER_TILE):
            if i & d == 0:
                a, b = v[i], v[i + d]
                nv[i] = jnp.where(lo, a, pltpu.roll(b, s, 1))
                nv[i + d] = jnp.where(lo, pltpu.roll(a, LANES - s, 1), b)
        v = nv
    return v


INPROJ_NC = 256


def _s5_inproj_body(x_ref, mod_ref, w_ref, o_ref, t_ref, *, tb):
    cb = tb // T_CHUNK
    sh = mod_ref[0, 0:1, :]
    sc1 = 1.0 + mod_ref[0, 1:2, :]
    h = (x_ref[...] * sc1 + sh).astype(_BF16)
    lane = lax.broadcasted_iota(jnp.int32, (cb, LANES), 1)
    for c in range(E_SSM // INPROJ_NC):
        u = _dot(h, w_ref[:, c * INPROJ_NC:(c + 1) * INPROJ_NC])
        for half in range(INPROJ_NC // LANES):
            j = c * (INPROJ_NC // LANES) + half
            uj = u[:, half * LANES:(half + 1) * LANES]
            for ch in range(cb):
                t_ref[j, ch * T_PITCH:ch * T_PITCH + T_CHUNK, :] = uj[ch * T_CHUNK:(ch + 1) * T_CHUNK]
            r = [t_ref[j, pl.ds(t, cb, stride=T_PITCH), :] for t in range(T_CHUNK)]
            for th in range(T_CHUNK // GROUPS_PER_TILE):
                w = _block_transpose8(r[th * GROUPS_PER_TILE:(th + 1) * GROUPS_PER_TILE], lane)
                for gg in range(GROUPS_PER_TILE):
                    o_ref[j * GROUPS_PER_TILE + gg, :, th * LANES:(th + 1) * LANES] = w[gg].astype(_BF16)


def _s5_inproj_call(x2d, mod3, w_u, *, tb, tokens_per_batch, mod_per_batch, name):
    nt = x2d.shape[0]
    bpb = tokens_per_batch // tb
    cb = tb // T_CHUNK
    mod_map = (lambda i: (i // bpb, 0, 0)) if mod_per_batch else (lambda i: (0, 0, 0))
    return pl.pallas_call(
        functools.partial(_s5_inproj_body, tb=tb),
        grid=(nt // tb,),
        in_specs=[
            pl.BlockSpec((tb, D_MODEL), lambda i: (i, 0)),
            pl.BlockSpec((1, 3, D_MODEL), mod_map),
            _const_spec((D_MODEL, E_SSM)),
        ],
        out_specs=pl.BlockSpec((N_GROUPS, cb, TP), lambda i: (0, i, 0)),
        out_shape=jax.ShapeDtypeStruct((N_GROUPS, nt // T_CHUNK, TP), _BF16),
        scratch_shapes=[pltpu.VMEM((N_TILES, cb * T_PITCH, LANES), _F32)],
        compiler_params=pltpu.CompilerParams(dimension_semantics=("arbitrary",),
                                             vmem_limit_bytes=VMEM_LIMIT),
        name=name,
    )(x2d, mod3, w_u)


def _cmul_add(ar, ai, hr, hi, sr, si):
    return ar * hr - ai * hi + sr, ar * hi + ai * hr + si


def _s5_core_body(u_ref, uc_ref, m_ref, bm_ref, cm_ref, coef_ref, y_ref,
                  s_ref, sc_ref, h_ref, *, n_chunks, n_ctx_chunks):
    half = SSM_STATE
    g8 = GROUPS_PER_TILE
    for gg in range(g8):
        s = _dot(u_ref[gg], bm_ref[gg])
        s_ref[0, pl.ds(gg, n_chunks, stride=g8), :] = s[:, 0:LANES]
        s_ref[1, pl.ds(gg, n_chunks, stride=g8), :] = s[:, LANES:]
        sc = _dot(uc_ref[gg, 0], bm_ref[gg])
        sc_ref[0, pl.ds(gg, n_ctx_chunks, stride=g8), :] = sc[:, 0:LANES]
        sc_ref[1, pl.ds(gg, n_ctx_chunks, stride=g8), :] = sc[:, LANES:]
    ar = coef_ref[0, 0]
    ai = coef_ref[0, 1]
    is_fwd = lax.broadcasted_iota(jnp.int32, (g8, LANES), 1) < half

    def step(src_ref, kf, kb, hr, hi):
        sr = jnp.where(is_fwd, src_ref[0, pl.ds(kf, g8), :], src_ref[0, pl.ds(kb, g8), :])
        si = jnp.where(is_fwd, src_ref[1, pl.ds(kf, g8), :], src_ref[1, pl.ds(kb, g8), :])
        return _cmul_add(ar, ai, hr, hi, sr, si)

    hr = jnp.zeros((g8, LANES), _F32)
    hi = jnp.zeros((g8, LANES), _F32)
    for k in range(n_ctx_chunks):
        hr, hi = step(sc_ref, k * g8, (n_ctx_chunks - 1 - k) * g8, hr, hi)

    def body(k, carry):
        hr, hi = carry
        kf = pl.multiple_of(k * g8, g8)
        kb = pl.multiple_of((n_chunks - 1 - k) * g8, g8)
        h_ref[0, pl.ds(kf, g8), 0:half] = hr[:, 0:half]
        h_ref[1, pl.ds(kf, g8), 0:half] = hi[:, 0:half]
        h_ref[0, pl.ds(kb, g8), half:] = hr[:, half:]
        h_ref[1, pl.ds(kb, g8), half:] = hi[:, half:]
        return step(s_ref, kf, kb, hr, hi)

    lax.fori_loop(0, n_chunks, body, (hr, hi), unroll=4)

    for gg in range(g8):
        hg = jnp.concatenate([h_ref[0, pl.ds(gg, n_chunks, stride=g8), :],
                              h_ref[1, pl.ds(gg, n_chunks, stride=g8), :]], axis=-1)
        y = _dot(u_ref[gg], m_ref[gg]) + _dot(hg.astype(_BF16), cm_ref[gg])
        y_ref[gg] = y.astype(_BF16)


def _s5_core_call(ug, ucg, m_op, b_op, c_op, coef, *, n_batch):
    g, nrow, _ = ug.shape
    n_chunks = nrow // n_batch
    n_ctx_chunks = ucg.shape[2]
    g8 = GROUPS_PER_TILE
    body = functools.partial(_s5_core_body, n_chunks=n_chunks, n_ctx_chunks=n_ctx_chunks)
    op_spec = pl.BlockSpec((g8, TP, TP), lambda b, j: (j, 0, 0))
    return pl.pallas_call(
        body,
        grid=(n_batch, g // g8),
        in_specs=[
            pl.BlockSpec((g8, n_chunks, TP), lambda b, j: (j, b, 0)),
            pl.BlockSpec((g8, 1, n_ctx_chunks, TP), lambda b, j: (j, b, 0, 0)),
            op_spec, op_spec, op_spec,
            pl.BlockSpec((1, 2, g8, LANES), lambda b, j: (j, 0, 0, 0)),
        ],
        out_specs=pl.BlockSpec((g8, n_chunks, TP), lambda b, j: (j, b, 0)),
        out_shape=jax.ShapeDtypeStruct((g, nrow, TP), _BF16),
        scratch_shapes=[
            pltpu.VMEM((2, n_chunks * g8, LANES), _F32),
            pltpu.VMEM((2, n_ctx_chunks * g8, LANES), _F32),
            pltpu.VMEM((2, n_chunks * g8, LANES), _F32),
        ],
        compiler_params=pltpu.CompilerParams(dimension_semantics=("arbitrary", "arbitrary"),
                                             vmem_limit_bytes=VMEM_LIMIT),
        name="s5_core",
    )(ug, ucg, m_op, b_op, c_op, coef)


OUT_NC = 512


def _s5_out_body(x_ref, y_ref, mod_ref, wz_ref, wglu_ref, bglu_ref, wout_ref, g_ref, b_ref, o_ref,
                 t_ref, act_ref, *, tb):
    cb = tb // T_CHUNK

    @pl.when(pl.program_id(0) == 0)
    def _():
        act_ref[...] = jnp.zeros_like(act_ref)

    sh = mod_ref[0, 0:1, :]
    sc1 = 1.0 + mod_ref[0, 1:2, :]
    gt = mod_ref[0, 2:3, :]
    x = x_ref[...]
    h = (x * sc1 + sh).astype(_BF16)
    acc = jnp.zeros((tb, D_MODEL), _F32)
    for nb in range(E_SSM // OUT_NC):
        c0 = nb * OUT_NC
        gl = _dot(act_ref[...], wglu_ref[:, c0:c0 + OUT_NC]) + bglu_ref[:, c0:c0 + OUT_NC]
        z = _dot(h, wz_ref[:, c0:c0 + OUT_NC])
        gated = act_ref[:, c0:c0 + OUT_NC].astype(_F32) * jax.nn.sigmoid(gl) * _silu(z)
        acc = acc + _dot(gated.astype(_BF16), wout_ref[c0:c0 + OUT_NC, :])
    r = DN_ALPHA * x + gt * acc
    o_ref[...] = _layernorm_rows(r, g_ref[...], b_ref[...])

    lane = lax.broadcasted_iota(jnp.int32, (cb, LANES), 1)
    for j in range(N_TILES):
        for th in range(T_CHUNK // GROUPS_PER_TILE):
            w = [_gelu_tanh(y_ref[j * GROUPS_PER_TILE + gg, :, th * LANES:(th + 1) * LANES].astype(_F32))
                 for gg in range(GROUPS_PER_TILE)]
            r = _block_transpose8(w, lane)
            for a in range(GROUPS_PER_TILE):
                t_ref[j, pl.ds(th * GROUPS_PER_TILE + a, cb, stride=T_PITCH), :] = r[a]
        for ch in range(cb):
            act_ref[ch * T_CHUNK:(ch + 1) * T_CHUNK, j * LANES:(j + 1) * LANES] = \
                t_ref[j, ch * T_PITCH:ch * T_PITCH + T_CHUNK, :].astype(_BF16)


def _s5_out_call(x2d, yg, mod3, w_z, w_glu, b_glu, w_out, ln_g, ln_b, *, tb, tokens_per_batch):
    nt = x2d.shape[0]
    bpb = tokens_per_batch // tb
    cb = tb // T_CHUNK
    nblk = nt // tb
    prev = lambda i: jnp.maximum(i - 1, 0)
    return pl.pallas_call(
        functools.partial(_s5_out_body, tb=tb),
        grid=(nblk + 1,),
        in_specs=[
            pl.BlockSpec((tb, D_MODEL), lambda i: (prev(i), 0)),
            pl.BlockSpec((N_GROUPS, cb, TP), lambda i: (0, jnp.minimum(i, nblk - 1), 0)),
            pl.BlockSpec((1, 3, D_MODEL), lambda i: (prev(i) // bpb, 0, 0)),
            _const_spec((D_MODEL, E_SSM)),
            _const_spec((E_SSM, E_SSM)),
            _const_spec((1, E_SSM)),
            _const_spec((E_SSM, D_MODEL)),
            _const_spec((1, D_MODEL)),
            _const_spec((1, D_MODEL)),
        ],
        out_specs=pl.BlockSpec((tb, D_MODEL), lambda i: (prev(i), 0)),
        out_shape=jax.ShapeDtypeStruct((nt, D_MODEL), _F32),
        scratch_shapes=[pltpu.VMEM((N_TILES, cb * T_PITCH, LANES), _F32),
                        pltpu.VMEM((tb, E_SSM), _BF16)],
        compiler_params=pltpu.CompilerParams(dimension_semantics=("arbitrary",),
                                             vmem_limit_bytes=VMEM_LIMIT),
        name="s5_out",
    )(x2d, yg, mod3, w_z, w_glu, b_glu, w_out, ln_g, ln_b)


def _s5_operators(lam_re, lam_im, log_step, b_re, b_im, c_re, c_im, d):
    t = T_CHUNK
    hi = lax.Precision.HIGHEST
    dt = jnp.exp(log_step)[..., None]
    mag = jnp.exp(lam_re * dt)
    a_r = mag * jnp.cos(lam_im * dt)
    a_i = mag * jnp.sin(lam_im * dt)
    q_r, q_i = a_r - 1.0, a_i
    den = lam_re * lam_re + lam_im * lam_im
    f_r = (q_r * lam_re + q_i * lam_im) / den
    f_i = (q_i * lam_re - q_r * lam_im) / den
    bb_r = f_r[..., None] * b_re - f_i[..., None] * b_im
    bb_i = f_r[..., None] * b_im + f_i[..., None] * b_re

    pw_r = [jnp.ones_like(a_r)]
    pw_i = [jnp.zeros_like(a_i)]
    for _ in range(t):
        pr_, pi_ = pw_r[-1], pw_i[-1]
        pw_r.append(pr_ * a_r - pi_ * a_i)
        pw_i.append(pr_ * a_i + pi_ * a_r)
    pw_r = jnp.stack(pw_r)
    pw_i = jnp.stack(pw_i)

    pg_r = pw_r.transpose(1, 2, 0, 3)
    pg_i = pw_i.transpose(1, 2, 0, 3)
    pn_r = pw_r.transpose(1, 2, 3, 0)
    pn_i = pw_i.transpose(1, 2, 3, 0)
    bt_r = bb_r.transpose(0, 1, 3, 2)[:, :, None]
    bt_i = bb_i.transpose(0, 1, 3, 2)[:, :, None]
    ct_r = c_re.transpose(0, 1, 3, 2)[:, :, :, None]
    ct_i = c_im.transpose(0, 1, 3, 2)[:, :, :, None]

    ab_r = pg_r[:, :, :t, None, :] * bt_r - pg_i[:, :, :t, None, :] * bt_i
    ab_i = pg_r[:, :, :t, None, :] * bt_i + pg_i[:, :, :t, None, :] * bt_r
    b_op = jnp.concatenate([ab_r[0][:, ::-1], ab_r[1], ab_i[0][:, ::-1], ab_i[1]], axis=-1)
    b_op = b_op.reshape(N_GROUPS, TP, 4 * SSM_STATE)

    ca_r = ct_r * pn_r[..., 1:, None] - ct_i * pn_i[..., 1:, None]
    ca_i = ct_r * pn_i[..., 1:, None] + ct_i * pn_r[..., 1:, None]
    c_op = jnp.concatenate([ca_r[0], ca_r[1][:, :, ::-1], -ca_i[0], -ca_i[1][:, :, ::-1]], axis=1)
    c_op = c_op.reshape(N_GROUPS, 4 * SSM_STATE, TP)

    abn_r = (pn_r[..., :t, None] * bb_r[:, :, :, None, :] - pn_i[..., :t, None] * bb_i[:, :, :, None, :])
    abn_i = (pn_r[..., :t, None] * bb_i[:, :, :, None, :] + pn_i[..., :t, None] * bb_r[:, :, :, None, :])
    abn_r = abn_r.reshape(2, N_GROUPS, SSM_STATE, TP)
    abn_i = abn_i.reshape(2, N_GROUPS, SSM_STATE, TP)
    kk = (jnp.einsum('rgpn,rgnx->rgpx', c_re, abn_r, precision=hi)
          - jnp.einsum('rgpn,rgnx->rgpx', c_im, abn_i, precision=hi))
    kq = kk.reshape(2, N_GROUPS, SSM_GROUP, t, SSM_GROUP).transpose(0, 1, 4, 3, 2)
    skip = (jnp.eye(SSM_GROUP, dtype=_F32)[None] * d.reshape(N_GROUPS, 1, SSM_GROUP))[:, :, None, :]
    center = kq[0][:, :, 0:1] + kq[1][:, :, 0:1] + skip
    kc = jnp.concatenate([kq[1][:, :, :0:-1], center, kq[0][:, :, 1:]], axis=2)
    m_op = jnp.stack([kc[:, :, t - 1 - s:2 * t - 1 - s] for s in range(t)], axis=1)
    m_op = m_op.reshape(N_GROUPS, TP, TP)

    cat = lambda x: jnp.concatenate([x[0], x[1]], axis=-1)
    coef = jnp.stack([cat(pw_r[t]), cat(pw_i[t])], axis=1)
    coef = coef.reshape(N_TILES, GROUPS_PER_TILE, 2, 2 * SSM_STATE).transpose(0, 2, 1, 3)
    return m_op.astype(_BF16), b_op.astype(_BF16), c_op.astype(_BF16), coef


def kernel(x, c, ctx, c_ctx, ada_w, ada_b, ln_g, ln_b, conv_w_in, conv_w, conv_w_out,
           ssm_w_in, ssm_lam_re, ssm_lam_im, ssm_log_step, ssm_b_re, ssm_b_im,
           ssm_c_re, ssm_c_im, ssm_d, ssm_w_glu, ssm_b_glu, ssm_w_out):
    bsz, seq, _ = x.shape
    ctx_len = ctx.shape[1]
    assert seq % 1024 == 0 and ctx_len % T_CHUNK == 0 and bsz < 8

    cs = jnp.zeros((8, D_MODEL), _F32).at[:bsz].set(c).at[bsz].set(c_ctx)
    mod = _ada_call(cs, ada_w, ada_b).reshape(DEPTH, 8, 3, D_MODEL)
    mod_x = [mod[i, :bsz] for i in range(DEPTH)]
    mod_c = [mod[i, bsz:bsz + 1] for i in range(DEPTH)]

    x2d = x.reshape(bsz * seq, D_MODEL)
    c2d = ctx.reshape(bsz * ctx_len, D_MODEL)
    g0, b0 = ln_g[0:1], ln_b[0:1]
    g1, b1 = ln_g[1:2], ln_b[1:2]

    w_in0 = conv_w_in[0].astype(_BF16)
    w_out0 = conv_w_out[0].astype(_BF16)
    x1 = _conv_layer_call(x2d, mod_x[0], w_in0, conv_w[0], w_out0, g0, b0, tb=1024,
                          tokens_per_batch=seq, period=GRID_W, n_horizontal=E_CONV // 2,
                          halo=True, mod_per_batch=True, name="conv_layer_x")
    c1 = _conv_layer_call(c2d, mod_c[0], w_in0, conv_w[0], w_out0, g0, b0, tb=ctx_len,
                          tokens_per_batch=ctx_len, period=ctx_len, n_horizontal=E_CONV,
                          halo=False, mod_per_batch=False, name="conv_layer_ctx")

    w_in1 = ssm_w_in[0].astype(_BF16)
    w_u, w_z = w_in1[:, :E_SSM], w_in1[:, E_SSM:]
    ug = _s5_inproj_call(x1, mod_x[1], w_u, tb=512, tokens_per_batch=seq, mod_per_batch=True,
                         name="s5_inproj_x")
    ucg = _s5_inproj_call(c1, mod_c[1], w_u, tb=ctx_len, tokens_per_batch=ctx_len,
                          mod_per_batch=False, name="s5_inproj_ctx")
    ucg = ucg.reshape(N_GROUPS, bsz, ctx_len // T_CHUNK, TP)

    m_op, b_op, c_op, coef = _s5_operators(ssm_lam_re[0], ssm_lam_im[0], ssm_log_step[0],
                                           ssm_b_re[0], ssm_b_im[0], ssm_c_re[0], ssm_c_im[0],
                                           ssm_d[0])
    yg = _s5_core_call(ug, ucg, m_op, b_op, c_op, coef, n_batch=bsz)

    out = _s5_out_call(x1, yg, mod_x[1], w_z, ssm_w_glu[0].astype(_BF16), ssm_b_glu[0:1],
                       ssm_w_out[0].astype(_BF16), g1, b1, tb=512, tokens_per_batch=seq)
    return out.reshape(bsz, seq, D_MODEL)
```

```python
import functools
import math

import jax
import jax.numpy as jnp
import numpy as np
from jax import lax
from jax.experimental import pallas as pl
from jax.experimental.pallas import tpu as pltpu

D_MODEL = 1024
GRID_W = 64
E_CONV = 2048
E_SSM = 2048
SSM_GROUP = 16
N_GROUPS = E_SSM // SSM_GROUP
SSM_STATE = 64
LN_EPS = 1e-5
DEPTH = 2
DN_ALPHA = (2 * DEPTH) ** 0.25

LANES = 128
SUBLANES = 8
T_CHUNK = 16
TP = T_CHUNK * SSM_GROUP
GROUPS_PER_TILE = LANES // SSM_GROUP
N_TILES = E_SSM // LANES
T_PITCH = 24
VMEM_LIMIT = 56 * 1024 * 1024

_F32 = jnp.float32
_BF16 = jnp.bfloat16


def _dot(a, b):
    return jnp.dot(a, b, preferred_element_type=_F32)


def _silu(z):
    return z * jax.nn.sigmoid(z)


def _gelu_tanh(x):
    c = math.sqrt(2.0 / math.pi)
    return x * (0.5 * (1.0 + jnp.tanh(c * (x + 0.044715 * (x * x * x)))))


def _layernorm_rows(r, g, b):
    mu = jnp.mean(r, axis=-1, keepdims=True)
    d = r - mu
    var = jnp.mean(d * d, axis=-1, keepdims=True)
    return d * lax.rsqrt(var + LN_EPS) * g + b


def _const_spec(shape, index=None):
    index = (0,) * len(shape) if index is None else index
    return pl.BlockSpec(shape, lambda *_: index, pipeline_mode=pl.Buffered(1))


def _mod_spec(layer, row_of_step):
    return pl.BlockSpec((None, None, 3, D_MODEL), lambda *i: (layer, row_of_step(*i), 0, 0))


def _ada_body(c_ref, w_ref, b_ref, o_ref):
    o_ref[0] = jnp.dot(_silu(c_ref[...]), w_ref[0], precision=lax.Precision.HIGHEST,
                       preferred_element_type=_F32) + b_ref[0]


def _ada_call(cs, ada_w, ada_b):
    depth = ada_w.shape[0]
    rows = cs.shape[0]
    return pl.pallas_call(
        _ada_body,
        grid=(depth, 3),
        in_specs=[
            pl.BlockSpec((rows, D_MODEL), lambda i, j: (0, 0)),
            pl.BlockSpec((1, D_MODEL, D_MODEL), lambda i, j: (i, 0, j)),
            pl.BlockSpec((1, 1, D_MODEL), lambda i, j: (i, 0, j)),
        ],
        out_specs=pl.BlockSpec((1, rows, D_MODEL), lambda i, j: (i, 0, j)),
        out_shape=jax.ShapeDtypeStruct((depth, rows, 3 * D_MODEL), _F32),
        compiler_params=pltpu.CompilerParams(dimension_semantics=("arbitrary", "arbitrary")),
        name="ada_mod",
    )(cs, ada_w, ada_b.reshape(depth, 1, 3 * D_MODEL))


CONV_CK = 256


def _conv_layer_body(*refs, tb, period, n_horizontal, blocks_per_batch, halo):
    if halo:
        x_ref, xp_ref, xn_ref, mod_ref, win_ref, cw_ref, wout_ref, g_ref, b_ref, o_ref = refs
    else:
        x_ref, mod_ref, win_ref, cw_ref, wout_ref, g_ref, b_ref, o_ref = refs
    sh = mod_ref[0:1, :]
    sc1 = 1.0 + mod_ref[1:2, :]
    gt = mod_ref[2:3, :]
    x = x_ref[...]
    h = (x * sc1 + sh).astype(_BF16)
    if halo:
        i = pl.program_id(0)
        jb = i % blocks_per_batch
        prev_ok = (jb != 0).astype(_F32)
        next_ok = (jb != blocks_per_batch - 1).astype(_F32)
        hp = (xp_ref[...] * sc1 + sh).astype(_BF16)
        hn = (xn_ref[...] * sc1 + sh).astype(_BF16)
    pos = lax.broadcasted_iota(jnp.int32, (tb, CONV_CK), 0) % period
    acc = jnp.zeros((tb, D_MODEL), _F32)
    for k in range(E_CONV // CONV_CK):
        c0 = k * CONV_CK
        bg = _dot(h, win_ref[:, c0:c0 + CONV_CK])
        u = _dot(h, win_ref[:, E_CONV + c0:E_CONV + c0 + CONV_CK]) * \
            _dot(h, win_ref[:, 2 * E_CONV + c0:2 * E_CONV + c0 + CONV_CK])
        z = _dot(h, win_ref[:, 3 * E_CONV + c0:3 * E_CONV + c0 + CONV_CK])
        w0 = cw_ref[0:1, c0:c0 + CONV_CK]
        w1 = cw_ref[1:2, c0:c0 + CONV_CK]
        w2 = cw_ref[2:3, c0:c0 + CONV_CK]
        if c0 < n_horizontal:
            um = jnp.where(pos != 0, pltpu.roll(u, 1, 0), 0.0)
            up = jnp.where(pos != period - 1, pltpu.roll(u, tb - 1, 0), 0.0)
        else:
            uph = _dot(hp, win_ref[:, E_CONV + c0:E_CONV + c0 + CONV_CK]) * \
                _dot(hp, win_ref[:, 2 * E_CONV + c0:2 * E_CONV + c0 + CONV_CK]) * prev_ok
            unh = _dot(hn, win_ref[:, E_CONV + c0:E_CONV + c0 + CONV_CK]) * \
                _dot(hn, win_ref[:, 2 * E_CONV + c0:2 * E_CONV + c0 + CONV_CK]) * next_ok
            um = jnp.concatenate([uph, u[:tb - GRID_W]], axis=0)
            up = jnp.concatenate([u[GRID_W:], unh], axis=0)
        yc = um * w0 + u * w1 + up * w2
        gated = (bg * yc * _silu(z)).astype(_BF16)
        acc = acc + _dot(gated, wout_ref[c0:c0 + CONV_CK, :])
    r = DN_ALPHA * x + gt * acc
    o_ref[...] = _layernorm_rows(r, g_ref[...], b_ref[...])


def _conv_layer_call(x2d, mod, w_in, conv_w, w_out, ln_g, ln_b, *, layer, conv_index, mod_row, tb,
                     tokens_per_batch, period, n_horizontal, halo, name):
    nt = x2d.shape[0]
    bpb = tokens_per_batch // tb
    hb = tb // GRID_W
    n_hblocks = nt // GRID_W
    in_specs = [pl.BlockSpec((tb, D_MODEL), lambda i: (i, 0))]
    args = [x2d]
    if halo:
        in_specs += [
            pl.BlockSpec((GRID_W, D_MODEL), lambda i: (jnp.maximum(i * hb - 1, 0), 0)),
            pl.BlockSpec((GRID_W, D_MODEL), lambda i: (jnp.minimum((i + 1) * hb, n_hblocks - 1), 0)),
        ]
        args += [x2d, x2d]
    in_specs += [
        _mod_spec(layer, mod_row),
        _const_spec((D_MODEL, 4 * E_CONV)),
        _const_spec((None, 3, E_CONV), (conv_index, 0, 0)),
        _const_spec((E_CONV, D_MODEL)),
        _const_spec((None, 1, D_MODEL), (layer, 0, 0)),
        _const_spec((None, 1, D_MODEL), (layer, 0, 0)),
    ]
    args += [mod, w_in, conv_w, w_out, ln_g, ln_b]
    body = functools.partial(_conv_layer_body, tb=tb, period=period, n_horizontal=n_horizontal,
                             blocks_per_batch=bpb, halo=halo)
    return pl.pallas_call(
        body,
        grid=(nt // tb,),
        in_specs=in_specs,
        out_specs=pl.BlockSpec((tb, D_MODEL), lambda i: (i, 0)),
        out_shape=jax.ShapeDtypeStruct((nt, D_MODEL), _F32),
        compiler_params=pltpu.CompilerParams(dimension_semantics=("arbitrary",),
                                             vmem_limit_bytes=VMEM_LIMIT),
        name=name,
    )(*args)


def _block_transpose8(v, lane):
    for d in (4, 2, 1):
        s = d * SSM_GROUP
        lo = (lane & s) == 0
        nv = list(v)
        for i in range(GROUPS_PER_TILE):
            if i & d == 0:
                a, b = v[i], v[i + d]
                nv[i] = jnp.where(lo, a, pltpu.roll(b, s, 1))
                nv[i + d] = jnp.where(lo, pltpu.roll(a, LANES - s, 1), b)
        v = nv
    return v


INPROJ_NC = 256


def _s5_inproj_body(x_ref, mod_ref, w_ref, o_ref, t_ref, *, tb):
    cb = tb // T_CHUNK
    sh = mod_ref[0:1, :]
    sc1 = 1.0 + mod_ref[1:2, :]
    h = (x_ref[...] * sc1 + sh).astype(_BF16)
    rows = 16
    lane = lax.broadcasted_iota(jnp.int32, (rows, LANES), 1)
    for c in range(E_SSM // INPROJ_NC):
        u = _dot(h, w_ref[:, c * INPROJ_NC:(c + 1) * INPROJ_NC])
        for half in range(INPROJ_NC // LANES):
            j = c * (INPROJ_NC // LANES) + half
            uj = u[:, half * LANES:(half + 1) * LANES]
            for ch in range(cb):
                t_ref[j, ch * T_PITCH:ch * T_PITCH + T_CHUNK, :] = uj[ch * T_CHUNK:(ch + 1) * T_CHUNK]
            for c0 in range(0, cb, rows):
                r = [t_ref[j, pl.ds(c0 * T_PITCH + t, rows, stride=T_PITCH), :] for t in range(T_CHUNK)]
                for th in range(T_CHUNK // GROUPS_PER_TILE):
                    w = _block_transpose8(r[th * GROUPS_PER_TILE:(th + 1) * GROUPS_PER_TILE], lane)
                    for gg in range(GROUPS_PER_TILE):
                        o_ref[j * GROUPS_PER_TILE + gg, c0:c0 + rows, th * LANES:(th + 1) * LANES] = \
                            w[gg].astype(_BF16)


def _s5_inproj_call(x2d, mod, w_in, *, layer, mod_row, tb, name):
    nt = x2d.shape[0]
    cb = tb // T_CHUNK
    return pl.pallas_call(
        functools.partial(_s5_inproj_body, tb=tb),
        grid=(nt // tb,),
        in_specs=[
            pl.BlockSpec((tb, D_MODEL), lambda i: (i, 0)),
            _mod_spec(layer, mod_row),
            _const_spec((D_MODEL, E_SSM), (0, 0)),
        ],
        out_specs=pl.BlockSpec((N_GROUPS, cb, TP), lambda i: (0, i, 0)),
        out_shape=jax.ShapeDtypeStruct((N_GROUPS, nt // T_CHUNK, TP), _BF16),
        scratch_shapes=[pltpu.VMEM((N_TILES, cb * T_PITCH, LANES), _F32)],
        compiler_params=pltpu.CompilerParams(dimension_semantics=("arbitrary",),
                                             vmem_limit_bytes=VMEM_LIMIT),
        name=name,
    )(x2d, mod, w_in)


def _cmul_add(ar, ai, hr, hi, sr, si):
    return ar * hr - ai * hi + sr, ar * hi + ai * hr + si


def _s5_core_body(u_ref, uc_ref, m_ref, bm_ref, cm_ref, coef_ref, y_ref,
                  s_ref, sc_ref, h_ref, *, n_chunks, n_ctx_chunks):
    half = SSM_STATE
    g8 = GROUPS_PER_TILE
    for gg in range(g8):
        s = _dot(u_ref[gg], bm_ref[gg])
        s_ref[0, pl.ds(gg, n_chunks, stride=g8), :] = s[:, 0:LANES]
        s_ref[1, pl.ds(gg, n_chunks, stride=g8), :] = s[:, LANES:]
        sc = _dot(uc_ref[gg, 0], bm_ref[gg])
        sc_ref[0, pl.ds(gg, n_ctx_chunks, stride=g8), :] = sc[:, 0:LANES]
        sc_ref[1, pl.ds(gg, n_ctx_chunks, stride=g8), :] = sc[:, LANES:]
    ar = coef_ref[0, 0]
    ai = coef_ref[0, 1]
    is_fwd = lax.broadcasted_iota(jnp.int32, (g8, LANES), 1) < half

    def step(src_ref, kf, kb, hr, hi):
        sr = jnp.where(is_fwd, src_ref[0, pl.ds(kf, g8), :], src_ref[0, pl.ds(kb, g8), :])
        si = jnp.where(is_fwd, src_ref[1, pl.ds(kf, g8), :], src_ref[1, pl.ds(kb, g8), :])
        return _cmul_add(ar, ai, hr, hi, sr, si)

    hr = jnp.zeros((g8, LANES), _F32)
    hi = jnp.zeros((g8, LANES), _F32)
    for k in range(n_ctx_chunks):
        hr, hi = step(sc_ref, k * g8, (n_ctx_chunks - 1 - k) * g8, hr, hi)

    def body(k, carry):
        hr, hi = carry
        kf = pl.multiple_of(k * g8, g8)
        kb = pl.multiple_of((n_chunks - 1 - k) * g8, g8)
        h_ref[0, pl.ds(kf, g8), 0:half] = hr[:, 0:half]
        h_ref[1, pl.ds(kf, g8), 0:half] = hi[:, 0:half]
        h_ref[0, pl.ds(kb, g8), half:] = hr[:, half:]
        h_ref[1, pl.ds(kb, g8), half:] = hi[:, half:]
        return step(s_ref, kf, kb, hr, hi)

    lax.fori_loop(0, n_chunks, body, (hr, hi), unroll=4)

    for gg in range(g8):
        hg = jnp.concatenate([h_ref[0, pl.ds(gg, n_chunks, stride=g8), :],
                              h_ref[1, pl.ds(gg, n_chunks, stride=g8), :]], axis=-1)
        y = _dot(u_ref[gg], m_ref[gg]) + _dot(hg.astype(_BF16), cm_ref[gg])
        y_ref[gg] = y.astype(_BF16)


def _s5_core_call(ug, ucg, m_op, b_op, c_op, coef, *, n_batch):
    g, nrow, _ = ug.shape
    n_chunks = nrow // n_batch
    n_ctx_chunks = ucg.shape[2]
    g8 = GROUPS_PER_TILE
    body = functools.partial(_s5_core_body, n_chunks=n_chunks, n_ctx_chunks=n_ctx_chunks)
    op_spec = pl.BlockSpec((g8, TP, TP), lambda b, j: (j, 0, 0))
    return pl.pallas_call(
        body,
        grid=(n_batch, g // g8),
        in_specs=[
            pl.BlockSpec((g8, n_chunks, TP), lambda b, j: (j, b, 0)),
            pl.BlockSpec((g8, 1, n_ctx_chunks, TP), lambda b, j: (j, b, 0, 0)),
            op_spec, op_spec, op_spec,
            pl.BlockSpec((1, 2, g8, LANES), lambda b, j: (j, 0, 0, 0)),
        ],
        out_specs=pl.BlockSpec((g8, n_chunks, TP), lambda b, j: (j, b, 0)),
        out_shape=jax.ShapeDtypeStruct((g, nrow, TP), _BF16),
        scratch_shapes=[
            pltpu.VMEM((2, n_chunks * g8, LANES), _F32),
            pltpu.VMEM((2, n_ctx_chunks * g8, LANES), _F32),
            pltpu.VMEM((2, n_chunks * g8, LANES), _F32),
        ],
        compiler_params=pltpu.CompilerParams(dimension_semantics=("arbitrary", "arbitrary"),
                                             vmem_limit_bytes=VMEM_LIMIT),
        name="s5_core",
    )(ug, ucg, m_op, b_op, c_op, coef)


OUT_NC = 512


def _s5_out_body(x_ref, y_ref, mod_ref, wz_ref, wglu_ref, bglu_ref, wout_ref, g_ref, b_ref, o_ref,
                 t_ref, act_ref, *, tb):
    cb = tb // T_CHUNK

    @pl.when(pl.program_id(0) == 0)
    def _():
        act_ref[...] = jnp.zeros_like(act_ref)

    sh = mod_ref[0:1, :]
    sc1 = 1.0 + mod_ref[1:2, :]
    gt = mod_ref[2:3, :]
    x = x_ref[...]
    h = (x * sc1 + sh).astype(_BF16)
    acc = jnp.zeros((tb, D_MODEL), _F32)
    for nb in range(E_SSM // OUT_NC):
        c0 = nb * OUT_NC
        gl = _dot(act_ref[...], wglu_ref[:, c0:c0 + OUT_NC]) + bglu_ref[:, c0:c0 + OUT_NC]
        z = _dot(h, wz_ref[:, c0:c0 + OUT_NC])
        gated = act_ref[:, c0:c0 + OUT_NC].astype(_F32) * jax.nn.sigmoid(gl) * _silu(z)
        acc = acc + _dot(gated.astype(_BF16), wout_ref[c0:c0 + OUT_NC, :])
    r = DN_ALPHA * x + gt * acc
    o_ref[...] = _layernorm_rows(r, g_ref[...], b_ref[...])

    lane = lax.broadcasted_iota(jnp.int32, (cb, LANES), 1)
    for j in range(N_TILES):
        for th in range(T_CHUNK // GROUPS_PER_TILE):
            w = [_gelu_tanh(y_ref[j * GROUPS_PER_TILE + gg, :, th * LANES:(th + 1) * LANES].astype(_F32))
                 for gg in range(GROUPS_PER_TILE)]
            r = _block_transpose8(w, lane)
            for a in range(GROUPS_PER_TILE):
                t_ref[j, pl.ds(th * GROUPS_PER_TILE + a, cb, stride=T_PITCH), :] = r[a]
        for ch in range(cb):
            act_ref[ch * T_CHUNK:(ch + 1) * T_CHUNK, j * LANES:(j + 1) * LANES] = \
                t_ref[j, ch * T_PITCH:ch * T_PITCH + T_CHUNK, :].astype(_BF16)


def _s5_out_call(x2d, yg, mod, w_in, w_glu, b_glu, w_out, ln_g, ln_b, *, layer, tb, tokens_per_batch):
    nt = x2d.shape[0]
    bpb = tokens_per_batch // tb
    cb = tb // T_CHUNK
    nblk = nt // tb
    prev = lambda i: jnp.maximum(i - 1, 0)
    return pl.pallas_call(
        functools.partial(_s5_out_body, tb=tb),
        grid=(nblk + 1,),
        in_specs=[
            pl.BlockSpec((tb, D_MODEL), lambda i: (prev(i), 0)),
            pl.BlockSpec((N_GROUPS, cb, TP), lambda i: (0, jnp.minimum(i, nblk - 1), 0)),
            _mod_spec(layer, lambda i: prev(i) // bpb),
            _const_spec((D_MODEL, E_SSM), (0, 1)),
            _const_spec((E_SSM, E_SSM)),
            _const_spec((None, 1, E_SSM), (0, 0, 0)),
            _const_spec((E_SSM, D_MODEL)),
            _const_spec((None, 1, D_MODEL), (layer, 0, 0)),
            _const_spec((None, 1, D_MODEL), (layer, 0, 0)),
        ],
        out_specs=pl.BlockSpec((tb, D_MODEL), lambda i: (prev(i), 0)),
        out_shape=jax.ShapeDtypeStruct((nt, D_MODEL), _F32),
        scratch_shapes=[pltpu.VMEM((N_TILES, cb * T_PITCH, LANES), _F32),
                        pltpu.VMEM((tb, E_SSM), _BF16)],
        compiler_params=pltpu.CompilerParams(dimension_semantics=("arbitrary",),
                                             vmem_limit_bytes=VMEM_LIMIT),
        name="s5_out",
    )(x2d, yg, mod, w_in, w_glu, b_glu, w_out, ln_g, ln_b)


def _s5_operators(lam_re, lam_im, log_step, b_re, b_im, c_re, c_im, d):
    t = T_CHUNK
    hi = lax.Precision.HIGHEST
    dt = jnp.exp(log_step)[..., None]
    rate = lam_re * dt
    theta = lam_im * dt
    a_r = jnp.exp(rate) * jnp.cos(theta)
    a_i = jnp.exp(rate) * jnp.sin(theta)
    q_r, q_i = a_r - 1.0, a_i
    den = lam_re * lam_re + lam_im * lam_im
    f_r = (q_r * lam_re + q_i * lam_im) / den
    f_i = (q_i * lam_re - q_r * lam_im) / den
    bb_r = f_r[..., None] * b_re - f_i[..., None] * b_im
    bb_i = f_r[..., None] * b_im + f_i[..., None] * b_re

    def powers(k_fwd, k_bwd, k_last):
        k = np.stack([k_fwd, k_bwd]).astype(np.float32)
        k = k[:, None, None, :] if k_last else k[:, None, :, None]
        ra = rate[..., None] if k_last else rate[:, :, None, :]
        th = theta[..., None] if k_last else theta[:, :, None, :]
        mag = jnp.exp(k * ra)
        return mag * jnp.cos(k * th), mag * jnp.sin(k * th)

    ar_t = np.arange(t)
    pb_r, pb_i = powers(t - 1 - ar_t, ar_t, False)
    bt_r = bb_r.transpose(0, 1, 3, 2)[:, :, None]
    bt_i = bb_i.transpose(0, 1, 3, 2)[:, :, None]
    ab_r = pb_r[:, :, :, None, :] * bt_r - pb_i[:, :, :, None, :] * bt_i
    ab_i = pb_r[:, :, :, None, :] * bt_i + pb_i[:, :, :, None, :] * bt_r
    b_op = jnp.concatenate([ab_r[0], ab_r[1], ab_i[0], ab_i[1]], axis=-1)
    b_op = b_op.reshape(N_GROUPS, TP, 4 * SSM_STATE)

    pc_r, pc_i = powers(ar_t + 1, t - ar_t, True)
    ct_r = c_re.transpose(0, 1, 3, 2)[:, :, :, None]
    ct_i = c_im.transpose(0, 1, 3, 2)[:, :, :, None]
    ca_r = ct_r * pc_r[..., None] - ct_i * pc_i[..., None]
    ca_i = ct_r * pc_i[..., None] + ct_i * pc_r[..., None]
    c_op = jnp.concatenate([ca_r[0], ca_r[1], -ca_i[0], -ca_i[1]], axis=1)
    c_op = c_op.reshape(N_GROUPS, 4 * SSM_STATE, TP)

    pk_r, pk_i = powers(ar_t, t - 1 - ar_t, True)
    abk_r = bb_r[..., None] * pk_r[:, :, :, None, :] - bb_i[..., None] * pk_i[:, :, :, None, :]
    abk_i = bb_r[..., None] * pk_i[:, :, :, None, :] + bb_i[..., None] * pk_r[:, :, :, None, :]
    abk_r = abk_r.reshape(2, N_GROUPS, SSM_STATE, TP)
    abk_i = abk_i.reshape(2, N_GROUPS, SSM_STATE, TP)
    kq = (jnp.einsum('rgnx,rgpn->rgxp', abk_r, c_re, precision=hi)
          - jnp.einsum('rgnx,rgpn->rgxp', abk_i, c_im, precision=hi))
    kq = kq.reshape(2, N_GROUPS, SSM_GROUP, t, SSM_GROUP)
    skip = (jnp.eye(SSM_GROUP, dtype=_F32)[None] * d.reshape(N_GROUPS, 1, SSM_GROUP))[:, :, None, :]
    center = kq[1][:, :, t - 1:t] + kq[0][:, :, 0:1] + skip
    kc = jnp.concatenate([kq[1][:, :, :t - 1], center, kq[0][:, :, 1:]], axis=2)
    sel = np.zeros((t, t, 2 * t - 1), np.float32)
    for s in range(t):
        sel[s, ar_t, ar_t - s + t - 1] = 1.0
    m_op = jnp.einsum('stj,gqjp->gsqtp', sel, kc, precision=hi).reshape(N_GROUPS, TP, TP)

    at_r, at_i = powers(np.array([t]), np.array([t]), False)
    coef = jnp.stack([jnp.concatenate([at_r[0], at_r[1]], axis=-1),
                      jnp.concatenate([at_i[0], at_i[1]], axis=-1)], axis=1)
    coef = coef.reshape(N_TILES, GROUPS_PER_TILE, 2, 2 * SSM_STATE).transpose(0, 2, 1, 3)
    return m_op.astype(_BF16), b_op.astype(_BF16), c_op.astype(_BF16), coef


def kernel(x, c, ctx, c_ctx, ada_w, ada_b, ln_g, ln_b, conv_w_in, conv_w, conv_w_out,
           ssm_w_in, ssm_lam_re, ssm_lam_im, ssm_log_step, ssm_b_re, ssm_b_im,
           ssm_c_re, ssm_c_im, ssm_d, ssm_w_glu, ssm_b_glu, ssm_w_out):
    bsz, seq, _ = x.shape
    ctx_len = ctx.shape[1]
    assert seq % 1024 == 0 and ctx_len % T_CHUNK == 0 and bsz < 8

    cs = jnp.concatenate([c, c_ctx[None], jnp.zeros((8 - bsz - 1, D_MODEL), _F32)], axis=0)
    mod = _ada_call(cs, ada_w, ada_b).reshape(DEPTH, 8, 3, D_MODEL)
    ctx_row = lambda i: bsz

    x2d = x.reshape(bsz * seq, D_MODEL)
    c2d = ctx.reshape(bsz * ctx_len, D_MODEL)
    ln_g3 = ln_g.reshape(DEPTH, 1, D_MODEL)
    ln_b3 = ln_b.reshape(DEPTH, 1, D_MODEL)

    w_in0 = conv_w_in[0].astype(_BF16)
    w_out0 = conv_w_out[0].astype(_BF16)
    conv_tb = 1024
    x1 = _conv_layer_call(x2d, mod, w_in0, conv_w, w_out0, ln_g3, ln_b3, layer=0, conv_index=0,
                          mod_row=lambda i: i // (seq // conv_tb), tb=conv_tb, tokens_per_batch=seq,
                          period=GRID_W, n_horizontal=E_CONV // 2, halo=True, name="conv_layer_x")
    c1 = _conv_layer_call(c2d, mod, w_in0, conv_w, w_out0, ln_g3, ln_b3, layer=0, conv_index=0,
                          mod_row=ctx_row, tb=ctx_len, tokens_per_batch=ctx_len,
                          period=ctx_len, n_horizontal=E_CONV, halo=False, name="conv_layer_ctx")

    w_in1 = ssm_w_in[0].astype(_BF16)
    inproj_tb = 512
    ug = _s5_inproj_call(x1, mod, w_in1, layer=1, mod_row=lambda i: i // (seq // inproj_tb),
                         tb=inproj_tb, name="s5_inproj_x")
    ucg = _s5_inproj_call(c1, mod, w_in1, layer=1, mod_row=ctx_row, tb=ctx_len, name="s5_inproj_ctx")
    ucg = ucg.reshape(N_GROUPS, bsz, ctx_len // T_CHUNK, TP)

    m_op, b_op, c_op, coef = _s5_operators(ssm_lam_re[0], ssm_lam_im[0], ssm_log_step[0],
                                           ssm_b_re[0], ssm_b_im[0], ssm_c_re[0], ssm_c_im[0],
                                           ssm_d[0])
    yg = _s5_core_call(ug, ucg, m_op, b_op, c_op, coef, n_batch=bsz)

    out = _s5_out_call(x1, yg, mod, w_in1, ssm_w_glu[0].astype(_BF16),
                       ssm_b_glu.reshape(-1, 1, E_SSM), ssm_w_out[0].astype(_BF16), ln_g3, ln_b3,
                       layer=1, tb=512, tokens_per_batch=seq)
    return out.reshape(bsz, seq, D_MODEL)
```

```python
import functools
import math

import jax
import jax.numpy as jnp
import numpy as np
from jax import lax
from jax.experimental import pallas as pl
from jax.experimental.pallas import tpu as pltpu

D_MODEL = 1024
GRID_W = 64
E_CONV = 2048
E_SSM = 2048
SSM_GROUP = 16
N_GROUPS = E_SSM // SSM_GROUP
SSM_STATE = 64
LN_EPS = 1e-5
DEPTH = 2
DN_ALPHA = (2 * DEPTH) ** 0.25

LANES = 128
SUBLANES = 8
T_CHUNK = 16
TP = T_CHUNK * SSM_GROUP
GROUPS_PER_TILE = LANES // SSM_GROUP
N_TILES = E_SSM // LANES
T_PITCH = 24
VMEM_LIMIT = 56 * 1024 * 1024

_F32 = jnp.float32
_BF16 = jnp.bfloat16


def _dot(a, b):
    return jnp.dot(a, b, preferred_element_type=_F32)


def _silu(z):
    return z * jax.nn.sigmoid(z)


def _gelu_tanh(x):
    c = math.sqrt(2.0 / math.pi)
    return x * (0.5 * (1.0 + jnp.tanh(c * (x + 0.044715 * (x * x * x)))))


def _layernorm_rows(r, g, b):
    mu = jnp.mean(r, axis=-1, keepdims=True)
    d = r - mu
    var = jnp.mean(d * d, axis=-1, keepdims=True)
    return d * lax.rsqrt(var + LN_EPS) * g + b


def _const_spec(shape, index=None):
    index = (0,) * len(shape) if index is None else index
    return pl.BlockSpec(shape, lambda *_: index, pipeline_mode=pl.Buffered(1))


def _mod_spec(layer, row_of_step):
    return pl.BlockSpec((None, None, 3, D_MODEL), lambda *i: (layer, row_of_step(*i), 0, 0))


CAST_BLOCK_ELEMS = 512 * 1024


def _cast_body(w_ref, o_ref):
    o_ref[...] = w_ref[...].astype(_BF16)


def _cast_bf16_call(w3d, index):
    _, rows, cols = w3d.shape
    rb = min(rows, CAST_BLOCK_ELEMS // cols)
    return pl.pallas_call(
        _cast_body,
        grid=(rows // rb,),
        in_specs=[pl.BlockSpec((None, rb, cols), lambda i: (index, i, 0))],
        out_specs=pl.BlockSpec((rb, cols), lambda i: (i, 0)),
        out_shape=jax.ShapeDtypeStruct((rows, cols), _BF16),
        compiler_params=pltpu.CompilerParams(dimension_semantics=("arbitrary",)),
        name="cast_bf16",
    )(w3d)


def _ada_body(c_ref, w_ref, b_ref, o_ref):
    o_ref[0] = jnp.dot(_silu(c_ref[...]), w_ref[0], precision=lax.Precision.HIGHEST,
                       preferred_element_type=_F32) + b_ref[0]


def _ada_call(cs, ada_w, ada_b):
    depth = ada_w.shape[0]
    rows = cs.shape[0]
    return pl.pallas_call(
        _ada_body,
        grid=(depth, 3),
        in_specs=[
            pl.BlockSpec((rows, D_MODEL), lambda i, j: (0, 0)),
            pl.BlockSpec((1, D_MODEL, D_MODEL), lambda i, j: (i, 0, j)),
            pl.BlockSpec((1, 1, D_MODEL), lambda i, j: (i, 0, j)),
        ],
        out_specs=pl.BlockSpec((1, rows, D_MODEL), lambda i, j: (i, 0, j)),
        out_shape=jax.ShapeDtypeStruct((depth, rows, 3 * D_MODEL), _F32),
        compiler_params=pltpu.CompilerParams(dimension_semantics=("arbitrary", "arbitrary")),
        name="ada_mod",
    )(cs, ada_w, ada_b.reshape(depth, 1, 3 * D_MODEL))


CONV_CK = 256


def _conv_layer_body(*refs, tb, period, n_horizontal, blocks_per_batch, halo):
    if halo:
        x_ref, xp_ref, xn_ref, mod_ref, win_ref, cw_ref, wout_ref, g_ref, b_ref, o_ref = refs
    else:
        x_ref, mod_ref, win_ref, cw_ref, wout_ref, g_ref, b_ref, o_ref = refs
    sh = mod_ref[0:1, :]
    sc1 = 1.0 + mod_ref[1:2, :]
    gt = mod_ref[2:3, :]
    x = x_ref[...]
    h = (x * sc1 + sh).astype(_BF16)
    if halo:
        i = pl.program_id(0)
        jb = i % blocks_per_batch
        prev_ok = (jb != 0).astype(_F32)
        next_ok = (jb != blocks_per_batch - 1).astype(_F32)
        hp = (xp_ref[...] * sc1 + sh).astype(_BF16)
        hn = (xn_ref[...] * sc1 + sh).astype(_BF16)
    pos = lax.broadcasted_iota(jnp.int32, (tb, CONV_CK), 0) % period
    acc = jnp.zeros((tb, D_MODEL), _F32)
    for k in range(E_CONV // CONV_CK):
        c0 = k * CONV_CK
        bg = _dot(h, win_ref[:, c0:c0 + CONV_CK])
        u = _dot(h, win_ref[:, E_CONV + c0:E_CONV + c0 + CONV_CK]) * \
            _dot(h, win_ref[:, 2 * E_CONV + c0:2 * E_CONV + c0 + CONV_CK])
        z = _dot(h, win_ref[:, 3 * E_CONV + c0:3 * E_CONV + c0 + CONV_CK])
        w0 = cw_ref[0:1, c0:c0 + CONV_CK]
        w1 = cw_ref[1:2, c0:c0 + CONV_CK]
        w2 = cw_ref[2:3, c0:c0 + CONV_CK]
        if c0 < n_horizontal:
            um = jnp.where(pos != 0, pltpu.roll(u, 1, 0), 0.0)
            up = jnp.where(pos != period - 1, pltpu.roll(u, tb - 1, 0), 0.0)
        else:
            uph = _dot(hp, win_ref[:, E_CONV + c0:E_CONV + c0 + CONV_CK]) * \
                _dot(hp, win_ref[:, 2 * E_CONV + c0:2 * E_CONV + c0 + CONV_CK]) * prev_ok
            unh = _dot(hn, win_ref[:, E_CONV + c0:E_CONV + c0 + CONV_CK]) * \
                _dot(hn, win_ref[:, 2 * E_CONV + c0:2 * E_CONV + c0 + CONV_CK]) * next_ok
            um = jnp.concatenate([uph, u[:tb - GRID_W]], axis=0)
            up = jnp.concatenate([u[GRID_W:], unh], axis=0)
        yc = um * w0 + u * w1 + up * w2
        gated = (bg * yc * _silu(z)).astype(_BF16)
        acc = acc + _dot(gated, wout_ref[c0:c0 + CONV_CK, :])
    r = DN_ALPHA * x + gt * acc
    o_ref[...] = _layernorm_rows(r, g_ref[...], b_ref[...])


def _conv_layer_call(x2d, mod, w_in, conv_w, w_out, ln_g, ln_b, *, layer, conv_index, mod_row, tb,
                     tokens_per_batch, period, n_horizontal, halo, name):
    nt = x2d.shape[0]
    bpb = tokens_per_batch // tb
    hb = tb // GRID_W
    n_hblocks = nt // GRID_W
    in_specs = [pl.BlockSpec((tb, D_MODEL), lambda i: (i, 0))]
    args = [x2d]
    if halo:
        in_specs += [
            pl.BlockSpec((GRID_W, D_MODEL), lambda i: (jnp.maximum(i * hb - 1, 0), 0)),
            pl.BlockSpec((GRID_W, D_MODEL), lambda i: (jnp.minimum((i + 1) * hb, n_hblocks - 1), 0)),
        ]
        args += [x2d, x2d]
    in_specs += [
        _mod_spec(layer, mod_row),
        _const_spec((D_MODEL, 4 * E_CONV)),
        _const_spec((None, 3, E_CONV), (conv_index, 0, 0)),
        _const_spec((E_CONV, D_MODEL)),
        _const_spec((None, 1, D_MODEL), (layer, 0, 0)),
        _const_spec((None, 1, D_MODEL), (layer, 0, 0)),
    ]
    args += [mod, w_in, conv_w, w_out, ln_g, ln_b]
    body = functools.partial(_conv_layer_body, tb=tb, period=period, n_horizontal=n_horizontal,
                             blocks_per_batch=bpb, halo=halo)
    return pl.pallas_call(
        body,
        grid=(nt // tb,),
        in_specs=in_specs,
        out_specs=pl.BlockSpec((tb, D_MODEL), lambda i: (i, 0)),
        out_shape=jax.ShapeDtypeStruct((nt, D_MODEL), _F32),
        compiler_params=pltpu.CompilerParams(dimension_semantics=("arbitrary",),
                                             vmem_limit_bytes=VMEM_LIMIT),
        name=name,
    )(*args)


def _block_transpose8(v, lane):
    for d in (4, 2, 1):
        s = d * SSM_GROUP
        lo = (lane & s) == 0
        nv = list(v)
        for i in range(GROUPS_PER_TILE):
            if i & d == 0:
                a, b = v[i], v[i + d]
                nv[i] = jnp.where(lo, a, pltpu.roll(b, s, 1))
                nv[i + d] = jnp.where(lo, pltpu.roll(a, LANES - s, 1), b)
        v = nv
    return v


INPROJ_NC = 256


def _s5_inproj_body(x_ref, mod_ref, w_ref, o_ref, t_ref, *, tb):
    cb = tb // T_CHUNK
    sh = mod_ref[0:1, :]
    sc1 = 1.0 + mod_ref[1:2, :]
    h = (x_ref[...] * sc1 + sh).astype(_BF16)
    rows = 16
    lane = lax.broadcasted_iota(jnp.int32, (rows, LANES), 1)
    for c in range(E_SSM // INPROJ_NC):
        u = _dot(h, w_ref[:, c * INPROJ_NC:(c + 1) * INPROJ_NC])
        for half in range(INPROJ_NC // LANES):
            j = c * (INPROJ_NC // LANES) + half
            uj = u[:, half * LANES:(half + 1) * LANES]
            for ch in range(cb):
                t_ref[j, ch * T_PITCH:ch * T_PITCH + T_CHUNK, :] = uj[ch * T_CHUNK:(ch + 1) * T_CHUNK]
            for c0 in range(0, cb, rows):
                r = [t_ref[j, pl.ds(c0 * T_PITCH + t, rows, stride=T_PITCH), :] for t in range(T_CHUNK)]
                for th in range(T_CHUNK // GROUPS_PER_TILE):
                    w = _block_transpose8(r[th * GROUPS_PER_TILE:(th + 1) * GROUPS_PER_TILE], lane)
                    for gg in range(GROUPS_PER_TILE):
                        o_ref[j * GROUPS_PER_TILE + gg, c0:c0 + rows, th * LANES:(th + 1) * LANES] = \
                            w[gg].astype(_BF16)


def _s5_inproj_call(x2d, mod, w_in, *, layer, mod_row, tb, name):
    nt = x2d.shape[0]
    cb = tb // T_CHUNK
    return pl.pallas_call(
        functools.partial(_s5_inproj_body, tb=tb),
        grid=(nt // tb,),
        in_specs=[
            pl.BlockSpec((tb, D_MODEL), lambda i: (i, 0)),
            _mod_spec(layer, mod_row),
            _const_spec((D_MODEL, E_SSM), (0, 0)),
        ],
        out_specs=pl.BlockSpec((N_GROUPS, cb, TP), lambda i: (0, i, 0)),
        out_shape=jax.ShapeDtypeStruct((N_GROUPS, nt // T_CHUNK, TP), _BF16),
        scratch_shapes=[pltpu.VMEM((N_TILES, cb * T_PITCH, LANES), _F32)],
        compiler_params=pltpu.CompilerParams(dimension_semantics=("arbitrary",),
                                             vmem_limit_bytes=VMEM_LIMIT),
        name=name,
    )(x2d, mod, w_in)


def _cmul_add(ar, ai, hr, hi, sr, si):
    return ar * hr - ai * hi + sr, ar * hi + ai * hr + si


def _s5_core_body(u_ref, uc_ref, m_ref, bm_ref, cm_ref, coef_ref, y_ref,
                  s_ref, sc_ref, h_ref, *, n_chunks, n_ctx_chunks):
    half = SSM_STATE
    g8 = GROUPS_PER_TILE
    for gg in range(g8):
        s = _dot(u_ref[gg], bm_ref[gg])
        s_ref[0, pl.ds(gg, n_chunks, stride=g8), :] = s[:, 0:LANES]
        s_ref[1, pl.ds(gg, n_chunks, stride=g8), :] = s[:, LANES:]
        sc = _dot(uc_ref[gg, 0], bm_ref[gg])
        sc_ref[0, pl.ds(gg, n_ctx_chunks, stride=g8), :] = sc[:, 0:LANES]
        sc_ref[1, pl.ds(gg, n_ctx_chunks, stride=g8), :] = sc[:, LANES:]
    ar = coef_ref[0, 0]
    ai = coef_ref[0, 1]
    is_fwd = lax.broadcasted_iota(jnp.int32, (g8, LANES), 1) < half

    def step(src_ref, kf, kb, hr, hi):
        sr = jnp.where(is_fwd, src_ref[0, pl.ds(kf, g8), :], src_ref[0, pl.ds(kb, g8), :])
        si = jnp.where(is_fwd, src_ref[1, pl.ds(kf, g8), :], src_ref[1, pl.ds(kb, g8), :])
        return _cmul_add(ar, ai, hr, hi, sr, si)

    hr = jnp.zeros((g8, LANES), _F32)
    hi = jnp.zeros((g8, LANES), _F32)
    for k in range(n_ctx_chunks):
        hr, hi = step(sc_ref, k * g8, (n_ctx_chunks - 1 - k) * g8, hr, hi)

    def body(k, carry):
        hr, hi = carry
        kf = pl.multiple_of(k * g8, g8)
        kb = pl.multiple_of((n_chunks - 1 - k) * g8, g8)
        h_ref[0, pl.ds(kf, g8), 0:half] = hr[:, 0:half]
        h_ref[1, pl.ds(kf, g8), 0:half] = hi[:, 0:half]
        h_ref[0, pl.ds(kb, g8), half:] = hr[:, half:]
        h_ref[1, pl.ds(kb, g8), half:] = hi[:, half:]
        return step(s_ref, kf, kb, hr, hi)

    lax.fori_loop(0, n_chunks, body, (hr, hi), unroll=4)

    for gg in range(g8):
        hg = jnp.concatenate([h_ref[0, pl.ds(gg, n_chunks, stride=g8), :],
                              h_ref[1, pl.ds(gg, n_chunks, stride=g8), :]], axis=-1)
        y = _dot(u_ref[gg], m_ref[gg]) + lax.dot_general(
            hg.astype(_BF16), cm_ref[gg], (((1,), (1,)), ((), ())), preferred_element_type=_F32)
        y_ref[gg] = y.astype(_BF16)


def _s5_core_call(ug, ucg, m_op, b_op, c_op, coef, *, n_batch):
    g, nrow, _ = ug.shape
    n_chunks = nrow // n_batch
    n_ctx_chunks = ucg.shape[2]
    g8 = GROUPS_PER_TILE
    body = functools.partial(_s5_core_body, n_chunks=n_chunks, n_ctx_chunks=n_ctx_chunks)
    op_spec = pl.BlockSpec((g8, TP, TP), lambda b, j: (j, 0, 0))
    return pl.pallas_call(
        body,
        grid=(n_batch, g // g8),
        in_specs=[
            pl.BlockSpec((g8, n_chunks, TP), lambda b, j: (j, b, 0)),
            pl.BlockSpec((g8, 1, n_ctx_chunks, TP), lambda b, j: (j, b, 0, 0)),
            op_spec, op_spec, op_spec,
            pl.BlockSpec((1, 2, g8, LANES), lambda b, j: (j, 0, 0, 0)),
        ],
        out_specs=pl.BlockSpec((g8, n_chunks, TP), lambda b, j: (j, b, 0)),
        out_shape=jax.ShapeDtypeStruct((g, nrow, TP), _BF16),
        scratch_shapes=[
            pltpu.VMEM((2, n_chunks * g8, LANES), _F32),
            pltpu.VMEM((2, n_ctx_chunks * g8, LANES), _F32),
            pltpu.VMEM((2, n_chunks * g8, LANES), _F32),
        ],
        compiler_params=pltpu.CompilerParams(dimension_semantics=("arbitrary", "arbitrary"),
                                             vmem_limit_bytes=VMEM_LIMIT),
        name="s5_core",
    )(ug, ucg, m_op, b_op, c_op, coef)


OUT_NC = 512


def _s5_out_body(x_ref, y_ref, mod_ref, wz_ref, wglu_ref, bglu_ref, wout_ref, g_ref, b_ref, o_ref,
                 t_ref, act_ref, *, tb):
    cb = tb // T_CHUNK

    @pl.when(pl.program_id(0) == 0)
    def _():
        act_ref[...] = jnp.zeros_like(act_ref)

    sh = mod_ref[0:1, :]
    sc1 = 1.0 + mod_ref[1:2, :]
    gt = mod_ref[2:3, :]
    x = x_ref[...]
    h = (x * sc1 + sh).astype(_BF16)
    acc = jnp.zeros((tb, D_MODEL), _F32)
    for nb in range(E_SSM // OUT_NC):
        c0 = nb * OUT_NC
        gl = _dot(act_ref[...], wglu_ref[:, c0:c0 + OUT_NC]) + bglu_ref[:, c0:c0 + OUT_NC]
        z = _dot(h, wz_ref[:, c0:c0 + OUT_NC])
        gated = act_ref[:, c0:c0 + OUT_NC].astype(_F32) * jax.nn.sigmoid(gl) * _silu(z)
        acc = acc + _dot(gated.astype(_BF16), wout_ref[c0:c0 + OUT_NC, :])
    r = DN_ALPHA * x + gt * acc
    o_ref[...] = _layernorm_rows(r, g_ref[...], b_ref[...])

    lane = lax.broadcasted_iota(jnp.int32, (cb, LANES), 1)
    for j in range(N_TILES):
        for th in range(T_CHUNK // GROUPS_PER_TILE):
            w = [_gelu_tanh(y_ref[j * GROUPS_PER_TILE + gg, :, th * LANES:(th + 1) * LANES].astype(_F32))
                 for gg in range(GROUPS_PER_TILE)]
            r = _block_transpose8(w, lane)
            for a in range(GROUPS_PER_TILE):
                t_ref[j, pl.ds(th * GROUPS_PER_TILE + a, cb, stride=T_PITCH), :] = r[a]
        for ch in range(cb):
            act_ref[ch * T_CHUNK:(ch + 1) * T_CHUNK, j * LANES:(j + 1) * LANES] = \
                t_ref[j, ch * T_PITCH:ch * T_PITCH + T_CHUNK, :].astype(_BF16)


def _s5_out_call(x2d, yg, mod, w_in, w_glu, b_glu, w_out, ln_g, ln_b, *, layer, tb, tokens_per_batch):
    nt = x2d.shape[0]
    bpb = tokens_per_batch // tb
    cb = tb // T_CHUNK
    nblk = nt // tb
    prev = lambda i: jnp.maximum(i - 1, 0)
    return pl.pallas_call(
        functools.partial(_s5_out_body, tb=tb),
        grid=(nblk + 1,),
        in_specs=[
            pl.BlockSpec((tb, D_MODEL), lambda i: (prev(i), 0)),
            pl.BlockSpec((N_GROUPS, cb, TP), lambda i: (0, jnp.minimum(i, nblk - 1), 0)),
            _mod_spec(layer, lambda i: prev(i) // bpb),
            _const_spec((D_MODEL, E_SSM), (0, 1)),
            _const_spec((E_SSM, E_SSM)),
            _const_spec((None, 1, E_SSM), (0, 0, 0)),
            _const_spec((E_SSM, D_MODEL)),
            _const_spec((None, 1, D_MODEL), (layer, 0, 0)),
            _const_spec((None, 1, D_MODEL), (layer, 0, 0)),
        ],
        out_specs=pl.BlockSpec((tb, D_MODEL), lambda i: (prev(i), 0)),
        out_shape=jax.ShapeDtypeStruct((nt, D_MODEL), _F32),
        scratch_shapes=[pltpu.VMEM((N_TILES, cb * T_PITCH, LANES), _F32),
                        pltpu.VMEM((tb, E_SSM), _BF16)],
        compiler_params=pltpu.CompilerParams(dimension_semantics=("arbitrary",),
                                             vmem_limit_bytes=VMEM_LIMIT),
        name="s5_out",
    )(x2d, yg, mod, w_in, w_glu, b_glu, w_out, ln_g, ln_b)


def _s5_expand_body(pb_ref, bt_ref, pc_ref, cr_ref, kc_ref, m_ref, b_ref, c_ref):
    t = T_CHUNK
    for gg in range(GROUPS_PER_TILE):
        bt_r, bt_i = bt_ref[gg, 0], bt_ref[gg, 1]
        cr_r, cr_i = cr_ref[gg, 0], cr_ref[gg, 1]
        taps = kc_ref[gg]
        for s in range(t):
            rows = slice(s * SSM_GROUP, (s + 1) * SSM_GROUP)
            p_r, p_i = pb_ref[gg, 0, s:s + 1, :], pb_ref[gg, 1, s:s + 1, :]
            b_ref[gg, rows, 0:LANES] = (p_r * bt_r - p_i * bt_i).astype(_BF16)
            b_ref[gg, rows, LANES:] = (p_r * bt_i + p_i * bt_r).astype(_BF16)
            p_r, p_i = pc_ref[gg, 0, s:s + 1, :], pc_ref[gg, 1, s:s + 1, :]
            c_ref[gg, rows, 0:LANES] = (p_r * cr_r - p_i * cr_i).astype(_BF16)
            c_ref[gg, rows, LANES:] = (-(p_r * cr_i + p_i * cr_r)).astype(_BF16)
            w0 = (t - 1 - s) * SSM_GROUP
            m_ref[gg, rows, :] = taps[:, w0:w0 + TP].astype(_BF16)


def _s5_expand_call(pb, bt, pc, cr, kc):
    g8 = GROUPS_PER_TILE
    tab = pl.BlockSpec((g8, 2, T_CHUNK, LANES), lambda j: (j, 0, 0, 0))
    op = pl.BlockSpec((g8, TP, TP), lambda j: (j, 0, 0))
    op_shape = jax.ShapeDtypeStruct((N_GROUPS, TP, TP), _BF16)
    return pl.pallas_call(
        _s5_expand_body,
        grid=(N_TILES,),
        in_specs=[tab, tab, tab, tab, pl.BlockSpec((g8, SSM_GROUP, 2 * TP), lambda j: (j, 0, 0))],
        out_specs=[op, op, op],
        out_shape=[op_shape, op_shape, op_shape],
        compiler_params=pltpu.CompilerParams(dimension_semantics=("arbitrary",)),
        name="s5_expand",
    )(pb, bt, pc, cr, kc)


def _s5_operators(lam_re, lam_im, log_step, b_re, b_im, c_re, c_im, d):
    t = T_CHUNK
    hi = lax.Precision.HIGHEST
    dt = jnp.exp(log_step)[..., None]
    rate = lam_re * dt
    theta = lam_im * dt
    a_r = jnp.exp(rate) * jnp.cos(theta)
    a_i = jnp.exp(rate) * jnp.sin(theta)
    q_r, q_i = a_r - 1.0, a_i
    den = lam_re * lam_re + lam_im * lam_im
    f_r = (q_r * lam_re + q_i * lam_im) / den
    f_i = (q_i * lam_re - q_r * lam_im) / den
    bb_r = f_r[..., None] * b_re - f_i[..., None] * b_im
    bb_i = f_r[..., None] * b_im + f_i[..., None] * b_re

    def powers(k_fwd, k_bwd, k_last):
        k = np.stack([k_fwd, k_bwd]).astype(np.float32)
        k = k[:, None, None, :] if k_last else k[:, None, :, None]
        ra = rate[..., None] if k_last else rate[:, :, None, :]
        th = theta[..., None] if k_last else theta[:, :, None, :]
        mag = jnp.exp(k * ra)
        return mag * jnp.cos(k * th), mag * jnp.sin(k * th)

    ar_t = np.arange(t)

    def pack(x_r, x_i):
        return jnp.stack([jnp.concatenate([x_r[0], x_r[1]], axis=-1),
                          jnp.concatenate([x_i[0], x_i[1]], axis=-1)], axis=1)

    pb = pack(*powers(t - 1 - ar_t, ar_t, False))
    pc = pack(*powers(ar_t + 1, t - ar_t, False))
    bt = pack(bb_r.transpose(0, 1, 3, 2), bb_i.transpose(0, 1, 3, 2))
    cr = pack(c_re, c_im)

    pk_r, pk_i = powers(ar_t, t - 1 - ar_t, True)
    abk_r = bb_r[..., None] * pk_r[:, :, :, None, :] - bb_i[..., None] * pk_i[:, :, :, None, :]
    abk_i = bb_r[..., None] * pk_i[:, :, :, None, :] + bb_i[..., None] * pk_r[:, :, :, None, :]
    abk_r = abk_r.reshape(2, N_GROUPS, SSM_STATE, TP)
    abk_i = abk_i.reshape(2, N_GROUPS, SSM_STATE, TP)
    kq = (jnp.einsum('rgnx,rgpn->rgxp', abk_r, c_re, precision=hi)
          - jnp.einsum('rgnx,rgpn->rgxp', abk_i, c_im, precision=hi))
    kq = kq.reshape(2, N_GROUPS, SSM_GROUP, t, SSM_GROUP)
    skip = (jnp.eye(SSM_GROUP, dtype=_F32)[None] * d.reshape(N_GROUPS, 1, SSM_GROUP))[:, :, None, :]
    center = kq[1][:, :, t - 1:t] + kq[0][:, :, 0:1] + skip
    kc = jnp.concatenate([kq[1][:, :, :t - 1], center, kq[0][:, :, 1:],
                          jnp.zeros_like(center)], axis=2)
    kc = kc.reshape(N_GROUPS, SSM_GROUP, 2 * TP)
    m_op, b_op, c_op_t = _s5_expand_call(pb, bt, pc, cr, kc)

    at_r, at_i = powers(np.array([t]), np.array([t]), False)
    coef = jnp.stack([jnp.concatenate([at_r[0], at_r[1]], axis=-1),
                      jnp.concatenate([at_i[0], at_i[1]], axis=-1)], axis=1)
    coef = coef.reshape(N_TILES, GROUPS_PER_TILE, 2, 2 * SSM_STATE).transpose(0, 2, 1, 3)
    return m_op, b_op, c_op_t, coef


def kernel(x, c, ctx, c_ctx, ada_w, ada_b, ln_g, ln_b, conv_w_in, conv_w, conv_w_out,
           ssm_w_in, ssm_lam_re, ssm_lam_im, ssm_log_step, ssm_b_re, ssm_b_im,
           ssm_c_re, ssm_c_im, ssm_d, ssm_w_glu, ssm_b_glu, ssm_w_out):
    bsz, seq, _ = x.shape
    ctx_len = ctx.shape[1]
    assert seq % 1024 == 0 and ctx_len % T_CHUNK == 0 and bsz < 8

    cs = jnp.concatenate([c, c_ctx[None], jnp.zeros((8 - bsz - 1, D_MODEL), _F32)], axis=0)
    mod = _ada_call(cs, ada_w, ada_b).reshape(DEPTH, 8, 3, D_MODEL)
    ctx_row = lambda i: bsz

    x2d = x.reshape(bsz * seq, D_MODEL)
    c2d = ctx.reshape(bsz * ctx_len, D_MODEL)
    ln_g3 = ln_g.reshape(DEPTH, 1, D_MODEL)
    ln_b3 = ln_b.reshape(DEPTH, 1, D_MODEL)

    w_in0 = _cast_bf16_call(conv_w_in, 0)
    w_out0 = _cast_bf16_call(conv_w_out, 0)
    conv_tb = 1024
    x1 = _conv_layer_call(x2d, mod, w_in0, conv_w, w_out0, ln_g3, ln_b3, layer=0, conv_index=0,
                          mod_row=lambda i: i // (seq // conv_tb), tb=conv_tb, tokens_per_batch=seq,
                          period=GRID_W, n_horizontal=E_CONV // 2, halo=True, name="conv_layer_x")
    c1 = _conv_layer_call(c2d, mod, w_in0, conv_w, w_out0, ln_g3, ln_b3, layer=0, conv_index=0,
                          mod_row=ctx_row, tb=ctx_len, tokens_per_batch=ctx_len,
                          period=ctx_len, n_horizontal=E_CONV, halo=False, name="conv_layer_ctx")

    w_in1 = _cast_bf16_call(ssm_w_in, 0)
    inproj_tb = 512
    ug = _s5_inproj_call(x1, mod, w_in1, layer=1, mod_row=lambda i: i // (seq // inproj_tb),
                         tb=inproj_tb, name="s5_inproj_x")
    ucg = _s5_inproj_call(c1, mod, w_in1, layer=1, mod_row=ctx_row, tb=ctx_len, name="s5_inproj_ctx")
    ucg = ucg.reshape(N_GROUPS, bsz, ctx_len // T_CHUNK, TP)

    m_op, b_op, c_op, coef = _s5_operators(ssm_lam_re[0], ssm_lam_im[0], ssm_log_step[0],
                                           ssm_b_re[0], ssm_b_im[0], ssm_c_re[0], ssm_c_im[0],
                                           ssm_d[0])
    yg = _s5_core_call(ug, ucg, m_op, b_op, c_op, coef, n_batch=bsz)

    out = _s5_out_call(x1, yg, mod, w_in1, _cast_bf16_call(ssm_w_glu, 0),
                       ssm_b_glu.reshape(-1, 1, E_SSM), _cast_bf16_call(ssm_w_out, 0), ln_g3, ln_b3,
                       layer=1, tb=512, tokens_per_batch=seq)
    return out.reshape(bsz, seq, D_MODEL)
```

```python
import functools
import math

import jax
import jax.numpy as jnp
import numpy as np
from jax import lax
from jax.experimental import pallas as pl
from jax.experimental.pallas import tpu as pltpu

D_MODEL = 1024
GRID_W = 64
E_CONV = 2048
E_SSM = 2048
SSM_GROUP = 16
N_GROUPS = E_SSM // SSM_GROUP
SSM_STATE = 64
LN_EPS = 1e-5
DEPTH = 2
DN_ALPHA = (2 * DEPTH) ** 0.25

LANES = 128
SUBLANES = 8
T_CHUNK = 16
TP = T_CHUNK * SSM_GROUP
GROUPS_PER_TILE = LANES // SSM_GROUP
N_TILES = E_SSM // LANES
T_PITCH = 24
VMEM_LIMIT = 56 * 1024 * 1024

_F32 = jnp.float32
_BF16 = jnp.bfloat16


def _dot(a, b):
    return jnp.dot(a, b, preferred_element_type=_F32)


def _silu(z):
    return z * jax.nn.sigmoid(z)


def _gelu_tanh(x):
    c = math.sqrt(2.0 / math.pi)
    return x * (0.5 * (1.0 + jnp.tanh(c * (x + 0.044715 * (x * x * x)))))


def _layernorm_rows(r, g, b):
    mu = jnp.mean(r, axis=-1, keepdims=True)
    d = r - mu
    var = jnp.mean(d * d, axis=-1, keepdims=True)
    return d * lax.rsqrt(var + LN_EPS) * g + b


def _const_spec(shape, index=None):
    index = (0,) * len(shape) if index is None else index
    return pl.BlockSpec(shape, lambda *_: index, pipeline_mode=pl.Buffered(1))


def _mod_spec(layer, row_of_step):
    return pl.BlockSpec((None, None, 3, D_MODEL), lambda *i: (layer, row_of_step(*i), 0, 0))


CAST_BLOCK_ELEMS = 512 * 1024


def _cast_body(w_ref, o_ref):
    o_ref[...] = w_ref[...].astype(_BF16)


def _cast_bf16_call(w3d, index):
    _, rows, cols = w3d.shape
    rb = min(rows, CAST_BLOCK_ELEMS // cols)
    return pl.pallas_call(
        _cast_body,
        grid=(rows // rb,),
        in_specs=[pl.BlockSpec((None, rb, cols), lambda i: (index, i, 0))],
        out_specs=pl.BlockSpec((rb, cols), lambda i: (i, 0)),
        out_shape=jax.ShapeDtypeStruct((rows, cols), _BF16),
        compiler_params=pltpu.CompilerParams(dimension_semantics=("arbitrary",)),
        name="cast_bf16",
    )(w3d)


def _ada_body(c_ref, w_ref, b_ref, o_ref):
    o_ref[0] = jnp.dot(_silu(c_ref[...]), w_ref[0], precision=lax.Precision.HIGHEST,
                       preferred_element_type=_F32) + b_ref[0]


def _ada_call(cs, ada_w, ada_b):
    depth = ada_w.shape[0]
    rows = cs.shape[0]
    return pl.pallas_call(
        _ada_body,
        grid=(depth, 3),
        in_specs=[
            pl.BlockSpec((rows, D_MODEL), lambda i, j: (0, 0)),
            pl.BlockSpec((1, D_MODEL, D_MODEL), lambda i, j: (i, 0, j)),
            pl.BlockSpec((1, 1, D_MODEL), lambda i, j: (i, 0, j)),
        ],
        out_specs=pl.BlockSpec((1, rows, D_MODEL), lambda i, j: (i, 0, j)),
        out_shape=jax.ShapeDtypeStruct((depth, rows, 3 * D_MODEL), _F32),
        compiler_params=pltpu.CompilerParams(dimension_semantics=("arbitrary", "arbitrary")),
        name="ada_mod",
    )(cs, ada_w, ada_b.reshape(depth, 1, 3 * D_MODEL))


CONV_CK = 256


def _conv_layer_body(*refs, tb, period, n_horizontal, blocks_per_batch, halo):
    if halo:
        x_ref, xp_ref, xn_ref, mod_ref, win_ref, cw_ref, wout_ref, g_ref, b_ref, o_ref = refs
    else:
        x_ref, mod_ref, win_ref, cw_ref, wout_ref, g_ref, b_ref, o_ref = refs
    sh = mod_ref[0:1, :]
    sc1 = 1.0 + mod_ref[1:2, :]
    gt = mod_ref[2:3, :]
    x = x_ref[...]
    h = (x * sc1 + sh).astype(_BF16)
    if halo:
        i = pl.program_id(0)
        jb = i % blocks_per_batch
        prev_ok = (jb != 0).astype(_F32)
        next_ok = (jb != blocks_per_batch - 1).astype(_F32)
        hp = ((xp_ref[...] * sc1 + sh) * prev_ok).astype(_BF16)
        hn = ((xn_ref[...] * sc1 + sh) * next_ok).astype(_BF16)
        h_ext = jnp.concatenate([hp, h, hn], axis=0)
    pos = lax.broadcasted_iota(jnp.int32, (tb, CONV_CK), 0) % period
    acc = jnp.zeros((tb, D_MODEL), _F32)
    for k in range(E_CONV // CONV_CK):
        c0 = k * CONV_CK
        bg = _dot(h, win_ref[:, c0:c0 + CONV_CK])
        z = _dot(h, win_ref[:, 3 * E_CONV + c0:3 * E_CONV + c0 + CONV_CK])
        w0 = cw_ref[0:1, c0:c0 + CONV_CK]
        w1 = cw_ref[1:2, c0:c0 + CONV_CK]
        w2 = cw_ref[2:3, c0:c0 + CONV_CK]
        if c0 < n_horizontal:
            u = _dot(h, win_ref[:, E_CONV + c0:E_CONV + c0 + CONV_CK]) * \
                _dot(h, win_ref[:, 2 * E_CONV + c0:2 * E_CONV + c0 + CONV_CK])
            um = jnp.where(pos != 0, pltpu.roll(u, 1, 0), 0.0)
            up = jnp.where(pos != period - 1, pltpu.roll(u, tb - 1, 0), 0.0)
        else:
            ue = _dot(h_ext, win_ref[:, E_CONV + c0:E_CONV + c0 + CONV_CK]) * \
                _dot(h_ext, win_ref[:, 2 * E_CONV + c0:2 * E_CONV + c0 + CONV_CK])
            um = ue[0:tb]
            u = ue[GRID_W:tb + GRID_W]
            up = ue[2 * GRID_W:tb + 2 * GRID_W]
        yc = um * w0 + u * w1 + up * w2
        gated = (bg * yc * _silu(z)).astype(_BF16)
        acc = acc + _dot(gated, wout_ref[c0:c0 + CONV_CK, :])
    r = DN_ALPHA * x + gt * acc
    o_ref[...] = _layernorm_rows(r, g_ref[...], b_ref[...])


def _conv_layer_call(x2d, mod, w_in, conv_w, w_out, ln_g, ln_b, *, layer, conv_index, mod_row, tb,
                     tokens_per_batch, period, n_horizontal, halo, name):
    nt = x2d.shape[0]
    bpb = tokens_per_batch // tb
    hb = tb // GRID_W
    n_hblocks = nt // GRID_W
    in_specs = [pl.BlockSpec((tb, D_MODEL), lambda i: (i, 0))]
    args = [x2d]
    if halo:
        in_specs += [
            pl.BlockSpec((GRID_W, D_MODEL), lambda i: (jnp.maximum(i * hb - 1, 0), 0)),
            pl.BlockSpec((GRID_W, D_MODEL), lambda i: (jnp.minimum((i + 1) * hb, n_hblocks - 1), 0)),
        ]
        args += [x2d, x2d]
    in_specs += [
        _mod_spec(layer, mod_row),
        _const_spec((D_MODEL, 4 * E_CONV)),
        _const_spec((None, 3, E_CONV), (conv_index, 0, 0)),
        _const_spec((E_CONV, D_MODEL)),
        _const_spec((None, 1, D_MODEL), (layer, 0, 0)),
        _const_spec((None, 1, D_MODEL), (layer, 0, 0)),
    ]
    args += [mod, w_in, conv_w, w_out, ln_g, ln_b]
    body = functools.partial(_conv_layer_body, tb=tb, period=period, n_horizontal=n_horizontal,
                             blocks_per_batch=bpb, halo=halo)
    return pl.pallas_call(
        body,
        grid=(nt // tb,),
        in_specs=in_specs,
        out_specs=pl.BlockSpec((tb, D_MODEL), lambda i: (i, 0)),
        out_shape=jax.ShapeDtypeStruct((nt, D_MODEL), _F32),
        compiler_params=pltpu.CompilerParams(dimension_semantics=("arbitrary",),
                                             vmem_limit_bytes=VMEM_LIMIT),
        name=name,
    )(*args)


def _block_transpose8(v, lane):
    for d in (4, 2, 1):
        s = d * SSM_GROUP
        lo = (lane & s) == 0
        nv = list(v)
        for i in range(GROUPS_PER_TILE):
            if i & d == 0:
                a, b = v[i], v[i + d]
                nv[i] = jnp.where(lo, a, pltpu.roll(b, s, 1))
                nv[i + d] = jnp.where(lo, pltpu.roll(a, LANES - s, 1), b)
        v = nv
    return v


INPROJ_NC = 256


def _s5_inproj_body(x_ref, mod_ref, w_ref, o_ref, t_ref, *, tb):
    cb = tb // T_CHUNK
    sh = mod_ref[0:1, :]
    sc1 = 1.0 + mod_ref[1:2, :]
    h = (x_ref[...] * sc1 + sh).astype(_BF16)
    rows = 16
    lane = lax.broadcasted_iota(jnp.int32, (rows, LANES), 1)
    for c in range(E_SSM // INPROJ_NC):
        u = _dot(h, w_ref[:, c * INPROJ_NC:(c + 1) * INPROJ_NC])
        for half in range(INPROJ_NC // LANES):
            j = c * (INPROJ_NC // LANES) + half
            uj = u[:, half * LANES:(half + 1) * LANES]
            for ch in range(cb):
                t_ref[j, ch * T_PITCH:ch * T_PITCH + T_CHUNK, :] = uj[ch * T_CHUNK:(ch + 1) * T_CHUNK]
            for c0 in range(0, cb, rows):
                r = [t_ref[j, pl.ds(c0 * T_PITCH + t, rows, stride=T_PITCH), :] for t in range(T_CHUNK)]
                for th in range(T_CHUNK // GROUPS_PER_TILE):
                    w = _block_transpose8(r[th * GROUPS_PER_TILE:(th + 1) * GROUPS_PER_TILE], lane)
                    for gg in range(GROUPS_PER_TILE):
                        o_ref[j * GROUPS_PER_TILE + gg, c0:c0 + rows, th * LANES:(th + 1) * LANES] = \
                            w[gg].astype(_BF16)


def _s5_inproj_call(x2d, mod, w_in, *, layer, mod_row, tb, name):
    nt = x2d.shape[0]
    cb = tb // T_CHUNK
    return pl.pallas_call(
        functools.partial(_s5_inproj_body, tb=tb),
        grid=(nt // tb,),
        in_specs=[
            pl.BlockSpec((tb, D_MODEL), lambda i: (i, 0)),
            _mod_spec(layer, mod_row),
            _const_spec((D_MODEL, E_SSM), (0, 0)),
        ],
        out_specs=pl.BlockSpec((N_GROUPS, cb, TP), lambda i: (0, i, 0)),
        out_shape=jax.ShapeDtypeStruct((N_GROUPS, nt // T_CHUNK, TP), _BF16),
        scratch_shapes=[pltpu.VMEM((N_TILES, cb * T_PITCH, LANES), _F32)],
        compiler_params=pltpu.CompilerParams(dimension_semantics=("arbitrary",),
                                             vmem_limit_bytes=VMEM_LIMIT),
        name=name,
    )(x2d, mod, w_in)


def _cmul_add(ar, ai, hr, hi, sr, si):
    return ar * hr - ai * hi + sr, ar * hi + ai * hr + si


def _s5_core_body(u_ref, uc_ref, m_ref, bm_ref, cm_ref, coef_ref, y_ref,
                  s_ref, sc_ref, h_ref, *, n_chunks, n_ctx_chunks):
    half = SSM_STATE
    g8 = GROUPS_PER_TILE
    for gg in range(g8):
        s = _dot(u_ref[gg], bm_ref[gg])
        s_ref[0, pl.ds(gg, n_chunks, stride=g8), :] = s[:, 0:LANES]
        s_ref[1, pl.ds(gg, n_chunks, stride=g8), :] = s[:, LANES:]
        sc = _dot(uc_ref[gg, 0], bm_ref[gg])
        sc_ref[0, pl.ds(gg, n_ctx_chunks, stride=g8), :] = sc[:, 0:LANES]
        sc_ref[1, pl.ds(gg, n_ctx_chunks, stride=g8), :] = sc[:, LANES:]
    ar = coef_ref[0, 0]
    ai = coef_ref[0, 1]
    is_fwd = lax.broadcasted_iota(jnp.int32, (g8, LANES), 1) < half

    def load_s(src_ref, kf, kb):
        return (jnp.where(is_fwd, src_ref[0, pl.ds(kf, g8), :], src_ref[0, pl.ds(kb, g8), :]),
                jnp.where(is_fwd, src_ref[1, pl.ds(kf, g8), :], src_ref[1, pl.ds(kb, g8), :]))

    hr = jnp.zeros((g8, LANES), _F32)
    hi = jnp.zeros((g8, LANES), _F32)
    for k in range(n_ctx_chunks):
        hr, hi = _cmul_add(ar, ai, hr, hi, *load_s(sc_ref, k * g8, (n_ctx_chunks - 1 - k) * g8))

    def body(k, carry):
        hr, hi = carry
        kf = pl.multiple_of(k * g8, g8)
        kb = pl.multiple_of((n_chunks - 1 - k) * g8, g8)
        h_ref[0, pl.ds(kf, g8), 0:half] = hr[:, 0:half]
        h_ref[1, pl.ds(kf, g8), 0:half] = hi[:, 0:half]
        h_ref[0, pl.ds(kb, g8), half:] = hr[:, half:]
        h_ref[1, pl.ds(kb, g8), half:] = hi[:, half:]
        return _cmul_add(ar, ai, hr, hi, *load_s(s_ref, kf, kb))

    lax.fori_loop(0, n_chunks, body, (hr, hi), unroll=4)

    for gg in range(g8):
        hg = jnp.concatenate([h_ref[0, pl.ds(gg, n_chunks, stride=g8), :],
                              h_ref[1, pl.ds(gg, n_chunks, stride=g8), :]], axis=-1)
        y = _dot(u_ref[gg], m_ref[gg]) + lax.dot_general(
            hg.astype(_BF16), cm_ref[gg], (((1,), (1,)), ((), ())), preferred_element_type=_F32)
        y_ref[gg] = y.astype(_BF16)


def _s5_core_call(ug, ucg, m_op, b_op, c_op, coef, *, n_batch):
    g, nrow, _ = ug.shape
    n_chunks = nrow // n_batch
    n_ctx_chunks = ucg.shape[2]
    g8 = GROUPS_PER_TILE
    body = functools.partial(_s5_core_body, n_chunks=n_chunks, n_ctx_chunks=n_ctx_chunks)
    op_spec = pl.BlockSpec((g8, TP, TP), lambda b, j: (j, 0, 0))
    return pl.pallas_call(
        body,
        grid=(n_batch, g // g8),
        in_specs=[
            pl.BlockSpec((g8, n_chunks, TP), lambda b, j: (j, b, 0)),
            pl.BlockSpec((g8, 1, n_ctx_chunks, TP), lambda b, j: (j, b, 0, 0)),
            op_spec, op_spec, op_spec,
            pl.BlockSpec((1, 2, g8, LANES), lambda b, j: (j, 0, 0, 0)),
        ],
        out_specs=pl.BlockSpec((g8, n_chunks, TP), lambda b, j: (j, b, 0)),
        out_shape=jax.ShapeDtypeStruct((g, nrow, TP), _BF16),
        scratch_shapes=[
            pltpu.VMEM((2, n_chunks * g8, LANES), _F32),
            pltpu.VMEM((2, n_ctx_chunks * g8, LANES), _F32),
            pltpu.VMEM((2, n_chunks * g8, LANES), _F32),
        ],
        compiler_params=pltpu.CompilerParams(dimension_semantics=("arbitrary", "arbitrary"),
                                             vmem_limit_bytes=VMEM_LIMIT),
        name="s5_core",
    )(ug, ucg, m_op, b_op, c_op, coef)


OUT_NC = 1024


def _s5_out_body(x_ref, y_ref, mod_ref, wz_ref, wglu_ref, bglu_ref, wout_ref, g_ref, b_ref, o_ref,
                 t_ref, act_ref, *, tb):
    cb = tb // T_CHUNK

    @pl.when(pl.program_id(0) == 0)
    def _():
        act_ref[...] = jnp.zeros_like(act_ref)

    sh = mod_ref[0:1, :]
    sc1 = 1.0 + mod_ref[1:2, :]
    gt = mod_ref[2:3, :]
    x = x_ref[...]
    h = (x * sc1 + sh).astype(_BF16)
    acc = jnp.zeros((tb, D_MODEL), _F32)
    for nb in range(E_SSM // OUT_NC):
        c0 = nb * OUT_NC
        gl = _dot(act_ref[...], wglu_ref[:, c0:c0 + OUT_NC]) + bglu_ref[:, c0:c0 + OUT_NC]
        z = _dot(h, wz_ref[:, c0:c0 + OUT_NC])
        gated = act_ref[:, c0:c0 + OUT_NC].astype(_F32) * jax.nn.sigmoid(gl) * _silu(z)
        acc = acc + _dot(gated.astype(_BF16), wout_ref[c0:c0 + OUT_NC, :])
    r = DN_ALPHA * x + gt * acc
    o_ref[...] = _layernorm_rows(r, g_ref[...], b_ref[...])

    lane = lax.broadcasted_iota(jnp.int32, (cb, LANES), 1)
    for j in range(N_TILES):
        for th in range(T_CHUNK // GROUPS_PER_TILE):
            w = [_gelu_tanh(y_ref[j * GROUPS_PER_TILE + gg, :, th * LANES:(th + 1) * LANES].astype(_F32))
                 for gg in range(GROUPS_PER_TILE)]
            r = _block_transpose8(w, lane)
            for a in range(GROUPS_PER_TILE):
                t_ref[j, pl.ds(th * GROUPS_PER_TILE + a, cb, stride=T_PITCH), :] = r[a]
        for ch in range(cb):
            act_ref[ch * T_CHUNK:(ch + 1) * T_CHUNK, j * LANES:(j + 1) * LANES] = \
                t_ref[j, ch * T_PITCH:ch * T_PITCH + T_CHUNK, :].astype(_BF16)


def _s5_out_call(x2d, yg, mod, w_in, w_glu, b_glu, w_out, ln_g, ln_b, *, layer, tb, tokens_per_batch):
    nt = x2d.shape[0]
    bpb = tokens_per_batch // tb
    cb = tb // T_CHUNK
    nblk = nt // tb
    prev = lambda i: jnp.maximum(i - 1, 0)
    return pl.pallas_call(
        functools.partial(_s5_out_body, tb=tb),
        grid=(nblk + 1,),
        in_specs=[
            pl.BlockSpec((tb, D_MODEL), lambda i: (prev(i), 0)),
            pl.BlockSpec((N_GROUPS, cb, TP), lambda i: (0, jnp.minimum(i, nblk - 1), 0)),
            _mod_spec(layer, lambda i: prev(i) // bpb),
            _const_spec((D_MODEL, E_SSM), (0, 1)),
            _const_spec((E_SSM, E_SSM)),
            _const_spec((None, 1, E_SSM), (0, 0, 0)),
            _const_spec((E_SSM, D_MODEL)),
            _const_spec((None, 1, D_MODEL), (layer, 0, 0)),
            _const_spec((None, 1, D_MODEL), (layer, 0, 0)),
        ],
        out_specs=pl.BlockSpec((tb, D_MODEL), lambda i: (prev(i), 0)),
        out_shape=jax.ShapeDtypeStruct((nt, D_MODEL), _F32),
        scratch_shapes=[pltpu.VMEM((N_TILES, cb * T_PITCH, LANES), _F32),
                        pltpu.VMEM((tb, E_SSM), _BF16)],
        compiler_params=pltpu.CompilerParams(dimension_semantics=("arbitrary",),
                                             vmem_limit_bytes=VMEM_LIMIT),
        name="s5_out",
    )(x2d, yg, mod, w_in, w_glu, b_glu, w_out, ln_g, ln_b)


def _s5_expand_body(pb_ref, bt_ref, pc_ref, cr_ref, kc_ref, m_ref, b_ref, c_ref):
    t = T_CHUNK
    for gg in range(GROUPS_PER_TILE):
        bt_r, bt_i = bt_ref[gg, 0], bt_ref[gg, 1]
        cr_r, cr_i = cr_ref[gg, 0], cr_ref[gg, 1]
        taps = kc_ref[gg]
        for s in range(t):
            rows = slice(s * SSM_GROUP, (s + 1) * SSM_GROUP)
            p_r, p_i = pb_ref[gg, 0, s:s + 1, :], pb_ref[gg, 1, s:s + 1, :]
            b_ref[gg, rows, 0:LANES] = (p_r * bt_r - p_i * bt_i).astype(_BF16)
            b_ref[gg, rows, LANES:] = (p_r * bt_i + p_i * bt_r).astype(_BF16)
            p_r, p_i = pc_ref[gg, 0, s:s + 1, :], pc_ref[gg, 1, s:s + 1, :]
            c_ref[gg, rows, 0:LANES] = (p_r * cr_r - p_i * cr_i).astype(_BF16)
            c_ref[gg, rows, LANES:] = (-(p_r * cr_i + p_i * cr_r)).astype(_BF16)
            w0 = (t - 1 - s) * SSM_GROUP
            m_ref[gg, rows, :] = taps[:, w0:w0 + TP].astype(_BF16)


def _s5_expand_call(pb, bt, pc, cr, kc):
    g8 = GROUPS_PER_TILE
    tab = pl.BlockSpec((g8, 2, T_CHUNK, LANES), lambda j: (j, 0, 0, 0))
    op = pl.BlockSpec((g8, TP, TP), lambda j: (j, 0, 0))
    op_shape = jax.ShapeDtypeStruct((N_GROUPS, TP, TP), _BF16)
    return pl.pallas_call(
        _s5_expand_body,
        grid=(N_TILES,),
        in_specs=[tab, tab, tab, tab, pl.BlockSpec((g8, SSM_GROUP, 2 * TP), lambda j: (j, 0, 0))],
        out_specs=[op, op, op],
        out_shape=[op_shape, op_shape, op_shape],
        compiler_params=pltpu.CompilerParams(dimension_semantics=("arbitrary",)),
        name="s5_expand",
    )(pb, bt, pc, cr, kc)


def _s5_operators(lam_re, lam_im, log_step, b_re, b_im, c_re, c_im, d):
    t = T_CHUNK
    hi = lax.Precision.HIGHEST
    dt = jnp.exp(log_step)[..., None]
    rate = lam_re * dt
    theta = lam_im * dt
    a_r = jnp.exp(rate) * jnp.cos(theta)
    a_i = jnp.exp(rate) * jnp.sin(theta)
    q_r, q_i = a_r - 1.0, a_i
    den = lam_re * lam_re + lam_im * lam_im
    f_r = (q_r * lam_re + q_i * lam_im) / den
    f_i = (q_i * lam_re - q_r * lam_im) / den
    bb_r = f_r[..., None] * b_re - f_i[..., None] * b_im
    bb_i = f_r[..., None] * b_im + f_i[..., None] * b_re

    def powers(k_fwd, k_bwd, k_last):
        k = np.stack([k_fwd, k_bwd]).astype(np.float32)
        k = k[:, None, None, :] if k_last else k[:, None, :, None]
        ra = rate[..., None] if k_last else rate[:, :, None, :]
        th = theta[..., None] if k_last else theta[:, :, None, :]
        mag = jnp.exp(k * ra)
        return mag * jnp.cos(k * th), mag * jnp.sin(k * th)

    ar_t = np.arange(t)

    def pack(x_r, x_i):
        return jnp.stack([jnp.concatenate([x_r[0], x_r[1]], axis=-1),
                          jnp.concatenate([x_i[0], x_i[1]], axis=-1)], axis=1)

    pb = pack(*powers(t - 1 - ar_t, ar_t, False))
    pc = pack(*powers(ar_t + 1, t - ar_t, False))
    bt = pack(bb_r.transpose(0, 1, 3, 2), bb_i.transpose(0, 1, 3, 2))
    cr = pack(c_re, c_im)

    pk_r, pk_i = powers(ar_t, t - 1 - ar_t, True)
    abk_r = bb_r[..., None] * pk_r[:, :, :, None, :] - bb_i[..., None] * pk_i[:, :, :, None, :]
    abk_i = bb_r[..., None] * pk_i[:, :, :, None, :] + bb_i[..., None] * pk_r[:, :, :, None, :]
    abk_r = abk_r.reshape(2, N_GROUPS, SSM_STATE, TP)
    abk_i = abk_i.reshape(2, N_GROUPS, SSM_STATE, TP)
    kq = (jnp.einsum('rgnx,rgpn->rgxp', abk_r, c_re, precision=hi)
          - jnp.einsum('rgnx,rgpn->rgxp', abk_i, c_im, precision=hi))
    kq = kq.reshape(2, N_GROUPS, SSM_GROUP, t, SSM_GROUP)
    skip = (jnp.eye(SSM_GROUP, dtype=_F32)[None] * d.reshape(N_GROUPS, 1, SSM_GROUP))[:, :, None, :]
    center = kq[1][:, :, t - 1:t] + kq[0][:, :, 0:1] + skip
    kc = jnp.concatenate([kq[1][:, :, :t - 1], center, kq[0][:, :, 1:],
                          jnp.zeros_like(center)], axis=2)
    kc = kc.reshape(N_GROUPS, SSM_GROUP, 2 * TP)
    m_op, b_op, c_op_t = _s5_expand_call(pb, bt, pc, cr, kc)

    at_r, at_i = powers(np.array([t]), np.array([t]), False)
    coef = jnp.stack([jnp.concatenate([at_r[0], at_r[1]], axis=-1),
                      jnp.concatenate([at_i[0], at_i[1]], axis=-1)], axis=1)
    coef = coef.reshape(N_TILES, GROUPS_PER_TILE, 2, 2 * SSM_STATE).transpose(0, 2, 1, 3)
    return m_op, b_op, c_op_t, coef


def kernel(x, c, ctx, c_ctx, ada_w, ada_b, ln_g, ln_b, conv_w_in, conv_w, conv_w_out,
           ssm_w_in, ssm_lam_re, ssm_lam_im, ssm_log_step, ssm_b_re, ssm_b_im,
           ssm_c_re, ssm_c_im, ssm_d, ssm_w_glu, ssm_b_glu, ssm_w_out):
    bsz, seq, _ = x.shape
    ctx_len = ctx.shape[1]
    assert seq % 1024 == 0 and ctx_len % T_CHUNK == 0 and bsz < 8

    cs = jnp.concatenate([c, c_ctx[None], jnp.zeros((8 - bsz - 1, D_MODEL), _F32)], axis=0)
    mod = _ada_call(cs, ada_w, ada_b).reshape(DEPTH, 8, 3, D_MODEL)
    ctx_row = lambda i: bsz

    x2d = x.reshape(bsz * seq, D_MODEL)
    c2d = ctx.reshape(bsz * ctx_len, D_MODEL)
    ln_g3 = ln_g.reshape(DEPTH, 1, D_MODEL)
    ln_b3 = ln_b.reshape(DEPTH, 1, D_MODEL)

    w_in0 = _cast_bf16_call(conv_w_in, 0)
    w_out0 = _cast_bf16_call(conv_w_out, 0)
    conv_tb = 1024
    x1 = _conv_layer_call(x2d, mod, w_in0, conv_w, w_out0, ln_g3, ln_b3, layer=0, conv_index=0,
                          mod_row=lambda i: i // (seq // conv_tb), tb=conv_tb, tokens_per_batch=seq,
                          period=GRID_W, n_horizontal=E_CONV // 2, halo=True, name="conv_layer_x")
    c1 = _conv_layer_call(c2d, mod, w_in0, conv_w, w_out0, ln_g3, ln_b3, layer=0, conv_index=0,
                          mod_row=ctx_row, tb=ctx_len, tokens_per_batch=ctx_len,
                          period=ctx_len, n_horizontal=E_CONV, halo=False, name="conv_layer_ctx")

    w_in1 = _cast_bf16_call(ssm_w_in, 0)
    inproj_tb = 512
    ug = _s5_inproj_call(x1, mod, w_in1, layer=1, mod_row=lambda i: i // (seq // inproj_tb),
                         tb=inproj_tb, name="s5_inproj_x")
    ucg = _s5_inproj_call(c1, mod, w_in1, layer=1, mod_row=ctx_row, tb=ctx_len, name="s5_inproj_ctx")
    ucg = ucg.reshape(N_GROUPS, bsz, ctx_len // T_CHUNK, TP)

    m_op, b_op, c_op, coef = _s5_operators(ssm_lam_re[0], ssm_lam_im[0], ssm_log_step[0],
                                           ssm_b_re[0], ssm_b_im[0], ssm_c_re[0], ssm_c_im[0],
                                           ssm_d[0])
    yg = _s5_core_call(ug, ucg, m_op, b_op, c_op, coef, n_batch=bsz)

    out = _s5_out_call(x1, yg, mod, w_in1, _cast_bf16_call(ssm_w_glu, 0),
                       ssm_b_glu.reshape(-1, 1, E_SSM), _cast_bf16_call(ssm_w_out, 0), ln_g3, ln_b3,
                       layer=1, tb=512, tokens_per_batch=seq)
    return out.reshape(bsz, seq, D_MODEL)
```

```python
import functools
import math

import jax
import jax.numpy as jnp
import numpy as np
from jax import lax
from jax.experimental import pallas as pl
from jax.experimental.pallas import tpu as pltpu

D_MODEL = 1024
GRID_W = 64
E_CONV = 2048
E_SSM = 2048
SSM_GROUP = 16
N_GROUPS = E_SSM // SSM_GROUP
SSM_STATE = 64
LN_EPS = 1e-5
DEPTH = 2
DN_ALPHA = (2 * DEPTH) ** 0.25

LANES = 128
SUBLANES = 8
T_CHUNK = 16
TP = T_CHUNK * SSM_GROUP
GROUPS_PER_TILE = LANES // SSM_GROUP
N_TILES = E_SSM // LANES
T_PITCH = 24
VMEM_LIMIT = 56 * 1024 * 1024

_F32 = jnp.float32
_BF16 = jnp.bfloat16


def _dot(a, b):
    return jnp.dot(a, b, preferred_element_type=_F32)


def _silu(z):
    return z * jax.nn.sigmoid(z)


def _gelu_tanh(x):
    c = math.sqrt(2.0 / math.pi)
    return x * (0.5 * (1.0 + jnp.tanh(c * (x + 0.044715 * (x * x * x)))))


def _layernorm_rows(r, g, b):
    mu = jnp.mean(r, axis=-1, keepdims=True)
    d = r - mu
    var = jnp.mean(d * d, axis=-1, keepdims=True)
    return d * lax.rsqrt(var + LN_EPS) * g + b


def _const_spec(shape, index=None):
    index = (0,) * len(shape) if index is None else index
    return pl.BlockSpec(shape, lambda *_: index, pipeline_mode=pl.Buffered(1))


def _mod_spec(layer, row_of_step):
    return pl.BlockSpec((None, None, 3, D_MODEL), lambda *i: (layer, row_of_step(*i), 0, 0))


CAST_BLOCK_ELEMS = 512 * 1024


def _cast_body(w_ref, o_ref):
    o_ref[...] = w_ref[...].astype(_BF16)


def _cast_bf16_call(w3d, index):
    _, rows, cols = w3d.shape
    rb = min(rows, CAST_BLOCK_ELEMS // cols)
    return pl.pallas_call(
        _cast_body,
        grid=(rows // rb,),
        in_specs=[pl.BlockSpec((None, rb, cols), lambda i: (index, i, 0))],
        out_specs=pl.BlockSpec((rb, cols), lambda i: (i, 0)),
        out_shape=jax.ShapeDtypeStruct((rows, cols), _BF16),
        compiler_params=pltpu.CompilerParams(dimension_semantics=("arbitrary",)),
        name="cast_bf16",
    )(w3d)


def _ada_body(c_ref, w_ref, b_ref, o_ref):
    o_ref[0] = jnp.dot(_silu(c_ref[...]), w_ref[0], precision=lax.Precision.HIGHEST,
                       preferred_element_type=_F32) + b_ref[0]


def _ada_call(cs, ada_w, ada_b):
    depth = ada_w.shape[0]
    rows = cs.shape[0]
    return pl.pallas_call(
        _ada_body,
        grid=(depth, 3),
        in_specs=[
            pl.BlockSpec((rows, D_MODEL), lambda i, j: (0, 0)),
            pl.BlockSpec((1, D_MODEL, D_MODEL), lambda i, j: (i, 0, j)),
            pl.BlockSpec((1, 1, D_MODEL), lambda i, j: (i, 0, j)),
        ],
        out_specs=pl.BlockSpec((1, rows, D_MODEL), lambda i, j: (i, 0, j)),
        out_shape=jax.ShapeDtypeStruct((depth, rows, 3 * D_MODEL), _F32),
        compiler_params=pltpu.CompilerParams(dimension_semantics=("arbitrary", "arbitrary")),
        name="ada_mod",
    )(cs, ada_w, ada_b.reshape(depth, 1, 3 * D_MODEL))


CONV_CK = 256


def _conv_layer_body(*refs, tb, period, n_horizontal, blocks_per_batch, halo):
    if halo:
        x_ref, xp_ref, xn_ref, mod_ref, win_ref, cw_ref, wout_ref, g_ref, b_ref, o_ref = refs
    else:
        x_ref, mod_ref, win_ref, cw_ref, wout_ref, g_ref, b_ref, o_ref = refs
    sh = mod_ref[0:1, :]
    sc1 = 1.0 + mod_ref[1:2, :]
    gt = mod_ref[2:3, :]
    x = x_ref[...]
    h = (x * sc1 + sh).astype(_BF16)
    if halo:
        i = pl.program_id(0)
        jb = i % blocks_per_batch
        prev_ok = (jb != 0).astype(_F32)
        next_ok = (jb != blocks_per_batch - 1).astype(_F32)
        hp = ((xp_ref[...] * sc1 + sh) * prev_ok).astype(_BF16)
        hn = ((xn_ref[...] * sc1 + sh) * next_ok).astype(_BF16)
        h_ext = jnp.concatenate([hp, h, hn], axis=0)
    pos = lax.broadcasted_iota(jnp.int32, (tb, CONV_CK), 0) % period
    acc = jnp.zeros((tb, D_MODEL), _F32)
    for k in range(E_CONV // CONV_CK):
        c0 = k * CONV_CK
        bg = _dot(h, win_ref[:, c0:c0 + CONV_CK])
        z = _dot(h, win_ref[:, 3 * E_CONV + c0:3 * E_CONV + c0 + CONV_CK])
        w0 = cw_ref[0:1, c0:c0 + CONV_CK]
        w1 = cw_ref[1:2, c0:c0 + CONV_CK]
        w2 = cw_ref[2:3, c0:c0 + CONV_CK]
        if c0 < n_horizontal:
            u = _dot(h, win_ref[:, E_CONV + c0:E_CONV + c0 + CONV_CK]) * \
                _dot(h, win_ref[:, 2 * E_CONV + c0:2 * E_CONV + c0 + CONV_CK])
            um = jnp.where(pos != 0, pltpu.roll(u, 1, 0), 0.0)
            up = jnp.where(pos != period - 1, pltpu.roll(u, tb - 1, 0), 0.0)
        else:
            ue = _dot(h_ext, win_ref[:, E_CONV + c0:E_CONV + c0 + CONV_CK]) * \
                _dot(h_ext, win_ref[:, 2 * E_CONV + c0:2 * E_CONV + c0 + CONV_CK])
            um = ue[0:tb]
            u = ue[GRID_W:tb + GRID_W]
            up = ue[2 * GRID_W:tb + 2 * GRID_W]
        yc = um * w0 + u * w1 + up * w2
        gated = (bg * yc * _silu(z)).astype(_BF16)
        acc = acc + _dot(gated, wout_ref[c0:c0 + CONV_CK, :])
    r = DN_ALPHA * x + gt * acc
    o_ref[...] = _layernorm_rows(r, g_ref[...], b_ref[...])


def _conv_layer_call(x2d, mod, w_in, conv_w, w_out, ln_g, ln_b, *, layer, conv_index, mod_row, tb,
                     tokens_per_batch, period, n_horizontal, halo, name):
    nt = x2d.shape[0]
    bpb = tokens_per_batch // tb
    hb = tb // GRID_W
    n_hblocks = nt // GRID_W
    in_specs = [pl.BlockSpec((tb, D_MODEL), lambda i: (i, 0))]
    args = [x2d]
    if halo:
        in_specs += [
            pl.BlockSpec((GRID_W, D_MODEL), lambda i: (jnp.maximum(i * hb - 1, 0), 0)),
            pl.BlockSpec((GRID_W, D_MODEL), lambda i: (jnp.minimum((i + 1) * hb, n_hblocks - 1), 0)),
        ]
        args += [x2d, x2d]
    in_specs += [
        _mod_spec(layer, mod_row),
        _const_spec((D_MODEL, 4 * E_CONV)),
        _const_spec((None, 3, E_CONV), (conv_index, 0, 0)),
        _const_spec((E_CONV, D_MODEL)),
        _const_spec((None, 1, D_MODEL), (layer, 0, 0)),
        _const_spec((None, 1, D_MODEL), (layer, 0, 0)),
    ]
    args += [mod, w_in, conv_w, w_out, ln_g, ln_b]
    body = functools.partial(_conv_layer_body, tb=tb, period=period, n_horizontal=n_horizontal,
                             blocks_per_batch=bpb, halo=halo)
    return pl.pallas_call(
        body,
        grid=(nt // tb,),
        in_specs=in_specs,
        out_specs=pl.BlockSpec((tb, D_MODEL), lambda i: (i, 0)),
        out_shape=jax.ShapeDtypeStruct((nt, D_MODEL), _F32),
        compiler_params=pltpu.CompilerParams(dimension_semantics=("arbitrary",),
                                             vmem_limit_bytes=VMEM_LIMIT),
        name=name,
    )(*args)


def _block_transpose8(v, lane):
    for d in (4, 2, 1):
        s = d * SSM_GROUP
        lo = (lane & s) == 0
        nv = list(v)
        for i in range(GROUPS_PER_TILE):
            if i & d == 0:
                a, b = v[i], v[i + d]
                nv[i] = jnp.where(lo, a, pltpu.roll(b, s, 1))
                nv[i + d] = jnp.where(lo, pltpu.roll(a, LANES - s, 1), b)
        v = nv
    return v


INPROJ_NC = 256


def _s5_inproj_body(x_ref, mod_ref, w_ref, o_ref, t_ref, *, tb):
    cb = tb // T_CHUNK
    sh = mod_ref[0:1, :]
    sc1 = 1.0 + mod_ref[1:2, :]
    h = (x_ref[...] * sc1 + sh).astype(_BF16)
    rows = 16
    lane = lax.broadcasted_iota(jnp.int32, (rows, LANES), 1)
    for c in range(E_SSM // INPROJ_NC):
        u = _dot(h, w_ref[:, c * INPROJ_NC:(c + 1) * INPROJ_NC])
        for half in range(INPROJ_NC // LANES):
            j = c * (INPROJ_NC // LANES) + half
            uj = u[:, half * LANES:(half + 1) * LANES]
            for ch in range(cb):
                t_ref[j, ch * T_PITCH:ch * T_PITCH + T_CHUNK, :] = uj[ch * T_CHUNK:(ch + 1) * T_CHUNK]
            for c0 in range(0, cb, rows):
                r = [t_ref[j, pl.ds(c0 * T_PITCH + t, rows, stride=T_PITCH), :] for t in range(T_CHUNK)]
                for th in range(T_CHUNK // GROUPS_PER_TILE):
                    w = _block_transpose8(r[th * GROUPS_PER_TILE:(th + 1) * GROUPS_PER_TILE], lane)
                    for gg in range(GROUPS_PER_TILE):
                        o_ref[j * GROUPS_PER_TILE + gg, c0:c0 + rows, th * LANES:(th + 1) * LANES] = \
                            w[gg].astype(_BF16)


def _s5_inproj_call(x2d, mod, w_in, *, layer, mod_row, tb, name):
    nt = x2d.shape[0]
    cb = tb // T_CHUNK
    return pl.pallas_call(
        functools.partial(_s5_inproj_body, tb=tb),
        grid=(nt // tb,),
        in_specs=[
            pl.BlockSpec((tb, D_MODEL), lambda i: (i, 0)),
            _mod_spec(layer, mod_row),
            _const_spec((D_MODEL, E_SSM), (0, 0)),
        ],
        out_specs=pl.BlockSpec((N_GROUPS, cb, TP), lambda i: (0, i, 0)),
        out_shape=jax.ShapeDtypeStruct((N_GROUPS, nt // T_CHUNK, TP), _BF16),
        scratch_shapes=[pltpu.VMEM((N_TILES, cb * T_PITCH, LANES), _F32)],
        compiler_params=pltpu.CompilerParams(dimension_semantics=("arbitrary",),
                                             vmem_limit_bytes=VMEM_LIMIT),
        name=name,
    )(x2d, mod, w_in)


SLAB_PAD_ROWS = 16


def _cmul_add(ar, ai, hr, hi, sr, si):
    return ar * hr - ai * hi + sr, ar * hi + ai * hr + si


def _s5_core_body(u_ref, uc_ref, m_ref, bm_ref, cm_ref, coef_ref, y_ref,
                  s_ref, sc_ref, h_ref, *, n_chunks, n_ctx_chunks):
    half = SSM_STATE
    g8 = GROUPS_PER_TILE
    for gg in range(g8):
        s = _dot(u_ref[gg], bm_ref[gg])
        s_ref[0, pl.ds(gg, n_chunks, stride=g8), :] = s[:, 0:LANES]
        s_ref[1, pl.ds(gg, n_chunks, stride=g8), :] = s[:, LANES:]
        sc = _dot(uc_ref[gg, 0], bm_ref[gg])
        sc_ref[0, pl.ds(gg, n_ctx_chunks, stride=g8), :] = sc[:, 0:LANES]
        sc_ref[1, pl.ds(gg, n_ctx_chunks, stride=g8), :] = sc[:, LANES:]

    ar = coef_ref[0, 0]
    ai = coef_ref[0, 1]
    is_fwd = lax.broadcasted_iota(jnp.int32, (g8, LANES), 1) < half

    def load_s(src_ref, kf, kb):
        return (jnp.where(is_fwd, src_ref[0, pl.ds(kf, g8), :], src_ref[0, pl.ds(kb, g8), :]),
                jnp.where(is_fwd, src_ref[1, pl.ds(kf, g8), :], src_ref[1, pl.ds(kb, g8), :]))

    hr = jnp.zeros((g8, LANES), _F32)
    hi = jnp.zeros((g8, LANES), _F32)
    for k in range(n_ctx_chunks):
        hr, hi = _cmul_add(ar, ai, hr, hi, *load_s(sc_ref, k * g8, (n_ctx_chunks - 1 - k) * g8))

    def body(k, carry):
        hr, hi = carry
        kf = pl.multiple_of(k * g8, g8)
        kb = pl.multiple_of((n_chunks - 1 - k) * g8, g8)
        h_ref[0, pl.ds(kf, g8), 0:half] = hr[:, 0:half]
        h_ref[1, pl.ds(kf, g8), 0:half] = hi[:, 0:half]
        h_ref[0, pl.ds(kb, g8), half:] = hr[:, half:]
        h_ref[1, pl.ds(kb, g8), half:] = hi[:, half:]
        return _cmul_add(ar, ai, hr, hi, *load_s(s_ref, kf, kb))

    lax.fori_loop(0, n_chunks, body, (hr, hi), unroll=4)

    for gg in range(g8):
        hg = jnp.concatenate([h_ref[0, pl.ds(gg, n_chunks, stride=g8), :],
                              h_ref[1, pl.ds(gg, n_chunks, stride=g8), :]], axis=-1)
        y = _dot(u_ref[gg], m_ref[gg]) + lax.dot_general(
            hg.astype(_BF16), cm_ref[gg], (((1,), (1,)), ((), ())), preferred_element_type=_F32)
        y_ref[gg] = y.astype(_BF16)


def _s5_core_call(ug, ucg, m_op, b_op, c_op, coef, *, n_batch):
    g, nrow, _ = ug.shape
    n_chunks = nrow // n_batch
    n_ctx_chunks = ucg.shape[2]
    g8 = GROUPS_PER_TILE
    body = functools.partial(_s5_core_body, n_chunks=n_chunks, n_ctx_chunks=n_ctx_chunks)
    op_spec = pl.BlockSpec((g8, TP, TP), lambda j, b: (j, 0, 0))
    return pl.pallas_call(
        body,
        grid=(g // g8, n_batch),
        in_specs=[
            pl.BlockSpec((g8, n_chunks, TP), lambda j, b: (j, b, 0)),
            pl.BlockSpec((g8, 1, n_ctx_chunks, TP), lambda j, b: (j, b, 0, 0)),
            op_spec, op_spec, op_spec,
            pl.BlockSpec((1, 2, g8, LANES), lambda j, b: (j, 0, 0, 0)),
        ],
        out_specs=pl.BlockSpec((g8, n_chunks, TP), lambda j, b: (j, b, 0)),
        out_shape=jax.ShapeDtypeStruct((g, nrow, TP), _BF16),
        scratch_shapes=[
            pltpu.VMEM((2, n_chunks * g8 + SLAB_PAD_ROWS, LANES), _F32),
            pltpu.VMEM((2, n_ctx_chunks * g8 + SLAB_PAD_ROWS, LANES), _F32),
            pltpu.VMEM((2, n_chunks * g8 + SLAB_PAD_ROWS, LANES), _F32),
        ],
        compiler_params=pltpu.CompilerParams(dimension_semantics=("arbitrary", "arbitrary"),
                                             vmem_limit_bytes=VMEM_LIMIT),
        name="s5_core",
    )(ug, ucg, m_op, b_op, c_op, coef)


OUT_NC = 1024
STAGE_SLOTS = 2


def _s5_out_body(x_ref, y_ref, mod_ref, wz_ref, wglu_ref, bglu_ref, wout_ref, g_ref, b_ref, o_ref,
                 t_ref, act_ref, *, tb):
    cb = tb // T_CHUNK

    @pl.when(pl.program_id(0) == 0)
    def _():
        act_ref[...] = jnp.zeros_like(act_ref)

    sh = mod_ref[0:1, :]
    sc1 = 1.0 + mod_ref[1:2, :]
    gt = mod_ref[2:3, :]
    x = x_ref[...]
    h = (x * sc1 + sh).astype(_BF16)
    acc = jnp.zeros((tb, D_MODEL), _F32)
    for nb in range(E_SSM // OUT_NC):
        c0 = nb * OUT_NC
        gl = _dot(act_ref[...], wglu_ref[:, c0:c0 + OUT_NC]) + bglu_ref[:, c0:c0 + OUT_NC]
        z = _dot(h, wz_ref[:, c0:c0 + OUT_NC])
        gated = act_ref[:, c0:c0 + OUT_NC].astype(_F32) * jax.nn.sigmoid(gl) * _silu(z)
        acc = acc + _dot(gated.astype(_BF16), wout_ref[c0:c0 + OUT_NC, :])
    r = DN_ALPHA * x + gt * acc
    o_ref[...] = _layernorm_rows(r, g_ref[...], b_ref[...])

    rows = 32
    lane = lax.broadcasted_iota(jnp.int32, (rows, LANES), 1)
    for j in range(N_TILES):
        slot = j % STAGE_SLOTS
        for c0 in range(0, cb, rows):
            for th in range(T_CHUNK // GROUPS_PER_TILE):
                w = [_gelu_tanh(y_ref[j * GROUPS_PER_TILE + gg, c0:c0 + rows,
                                      th * LANES:(th + 1) * LANES].astype(_F32))
                     for gg in range(GROUPS_PER_TILE)]
                r = _block_transpose8(w, lane)
                for a in range(GROUPS_PER_TILE):
                    t_ref[slot, pl.ds(c0 * T_PITCH + th * GROUPS_PER_TILE + a, rows, stride=T_PITCH), :] = r[a]
        for ch in range(cb):
            act_ref[ch * T_CHUNK:(ch + 1) * T_CHUNK, j * LANES:(j + 1) * LANES] = \
                t_ref[slot, ch * T_PITCH:ch * T_PITCH + T_CHUNK, :].astype(_BF16)


def _s5_out_call(x2d, yg, mod, w_in, w_glu, b_glu, w_out, ln_g, ln_b, *, layer, tb, tokens_per_batch):
    nt = x2d.shape[0]
    bpb = tokens_per_batch // tb
    cb = tb // T_CHUNK
    nblk = nt // tb
    prev = lambda i: jnp.maximum(i - 1, 0)
    return pl.pallas_call(
        functools.partial(_s5_out_body, tb=tb),
        grid=(nblk + 1,),
        in_specs=[
            pl.BlockSpec((tb, D_MODEL), lambda i: (prev(i), 0)),
            pl.BlockSpec((N_GROUPS, cb, TP), lambda i: (0, jnp.minimum(i, nblk - 1), 0)),
            _mod_spec(layer, lambda i: prev(i) // bpb),
            _const_spec((D_MODEL, E_SSM), (0, 1)),
            _const_spec((E_SSM, E_SSM)),
            _const_spec((None, 1, E_SSM), (0, 0, 0)),
            _const_spec((E_SSM, D_MODEL)),
            _const_spec((None, 1, D_MODEL), (layer, 0, 0)),
            _const_spec((None, 1, D_MODEL), (layer, 0, 0)),
        ],
        out_specs=pl.BlockSpec((tb, D_MODEL), lambda i: (prev(i), 0)),
        out_shape=jax.ShapeDtypeStruct((nt, D_MODEL), _F32),
        scratch_shapes=[pltpu.VMEM((STAGE_SLOTS, cb * T_PITCH, LANES), _F32),
                        pltpu.VMEM((tb, E_SSM), _BF16)],
        compiler_params=pltpu.CompilerParams(dimension_semantics=("arbitrary",),
                                             vmem_limit_bytes=VMEM_LIMIT),
        name="s5_out",
    )(x2d, yg, mod, w_in, w_glu, b_glu, w_out, ln_g, ln_b)


def _s5_expand_body(pb_ref, bt_ref, pc_ref, cr_ref, kc_ref, m_ref, b_ref, c_ref):
    t = T_CHUNK
    for gg in range(GROUPS_PER_TILE):
        bt_r, bt_i = bt_ref[gg, 0], bt_ref[gg, 1]
        cr_r, cr_i = cr_ref[gg, 0], cr_ref[gg, 1]
        taps = kc_ref[gg]
        for s in range(t):
            rows = slice(s * SSM_GROUP, (s + 1) * SSM_GROUP)
            p_r, p_i = pb_ref[gg, 0, s:s + 1, :], pb_ref[gg, 1, s:s + 1, :]
            b_ref[gg, rows, 0:LANES] = (p_r * bt_r - p_i * bt_i).astype(_BF16)
            b_ref[gg, rows, LANES:] = (p_r * bt_i + p_i * bt_r).astype(_BF16)
            p_r, p_i = pc_ref[gg, 0, s:s + 1, :], pc_ref[gg, 1, s:s + 1, :]
            c_ref[gg, rows, 0:LANES] = (p_r * cr_r - p_i * cr_i).astype(_BF16)
            c_ref[gg, rows, LANES:] = (-(p_r * cr_i + p_i * cr_r)).astype(_BF16)
            w0 = (t - 1 - s) * SSM_GROUP
            m_ref[gg, rows, :] = taps[:, w0:w0 + TP].astype(_BF16)


def _s5_expand_call(pb, bt, pc, cr, kc):
    g8 = GROUPS_PER_TILE
    tab = pl.BlockSpec((g8, 2, T_CHUNK, LANES), lambda j: (j, 0, 0, 0))
    op = pl.BlockSpec((g8, TP, TP), lambda j: (j, 0, 0))
    op_shape = jax.ShapeDtypeStruct((N_GROUPS, TP, TP), _BF16)
    return pl.pallas_call(
        _s5_expand_body,
        grid=(N_TILES,),
        in_specs=[tab, tab, tab, tab, pl.BlockSpec((g8, SSM_GROUP, 2 * TP), lambda j: (j, 0, 0))],
        out_specs=[op, op, op],
        out_shape=[op_shape, op_shape, op_shape],
        compiler_params=pltpu.CompilerParams(dimension_semantics=("arbitrary",)),
        name="s5_expand",
    )(pb, bt, pc, cr, kc)


def _s5_operators(lam_re, lam_im, log_step, b_re, b_im, c_re, c_im, d):
    t = T_CHUNK
    dt = jnp.exp(log_step)[..., None]
    rate = lam_re * dt
    theta = lam_im * dt
    a_r = jnp.exp(rate) * jnp.cos(theta)
    a_i = jnp.exp(rate) * jnp.sin(theta)
    q_r, q_i = a_r - 1.0, a_i
    den = lam_re * lam_re + lam_im * lam_im
    f_r = (q_r * lam_re + q_i * lam_im) / den
    f_i = (q_i * lam_re - q_r * lam_im) / den
    bb_r = f_r[..., None] * b_re - f_i[..., None] * b_im
    bb_i = f_r[..., None] * b_im + f_i[..., None] * b_re

    def powers(k_fwd, k_bwd, k_last):
        k = np.stack([k_fwd, k_bwd]).astype(np.float32)
        k = k[:, None, None, :] if k_last else k[:, None, :, None]
        ra = rate[..., None] if k_last else rate[:, :, None, :]
        th = theta[..., None] if k_last else theta[:, :, None, :]
        mag = jnp.exp(k * ra)
        return mag * jnp.cos(k * th), mag * jnp.sin(k * th)

    ar_t = np.arange(t)

    def pack(x_r, x_i):
        return jnp.stack([jnp.concatenate([x_r[0], x_r[1]], axis=-1),
                          jnp.concatenate([x_i[0], x_i[1]], axis=-1)], axis=1)

    pb = pack(*powers(t - 1 - ar_t, ar_t, False))
    pc = pack(*powers(ar_t + 1, t - ar_t, False))
    bt = pack(bb_r.transpose(0, 1, 3, 2), bb_i.transpose(0, 1, 3, 2))
    cr = pack(c_re, c_im)

    pk_r, pk_i = powers(ar_t, t - 1 - ar_t, True)
    abk_r = bb_r[..., None] * pk_r[:, :, :, None, :] - bb_i[..., None] * pk_i[:, :, :, None, :]
    abk_i = bb_r[..., None] * pk_i[:, :, :, None, :] + bb_i[..., None] * pk_r[:, :, :, None, :]
    abk_r = abk_r.reshape(2, N_GROUPS, SSM_STATE, TP).astype(_BF16)
    abk_i = abk_i.reshape(2, N_GROUPS, SSM_STATE, TP).astype(_BF16)
    kq = (jnp.einsum('rgnx,rgpn->rgxp', abk_r, c_re.astype(_BF16), preferred_element_type=_F32)
          - jnp.einsum('rgnx,rgpn->rgxp', abk_i, c_im.astype(_BF16), preferred_element_type=_F32))
    kq = kq.reshape(2, N_GROUPS, SSM_GROUP, t, SSM_GROUP)
    skip = (jnp.eye(SSM_GROUP, dtype=_F32)[None] * d.reshape(N_GROUPS, 1, SSM_GROUP))[:, :, None, :]
    center = kq[1][:, :, t - 1:t] + kq[0][:, :, 0:1] + skip
    kc = jnp.concatenate([kq[1][:, :, :t - 1], center, kq[0][:, :, 1:],
                          jnp.zeros_like(center)], axis=2)
    kc = kc.reshape(N_GROUPS, SSM_GROUP, 2 * TP)
    m_op, b_op, c_op_t = _s5_expand_call(pb, bt, pc, cr, kc)

    at_r, at_i = powers(np.array([t]), np.array([t]), False)
    coef = jnp.stack([jnp.concatenate([at_r[0], at_r[1]], axis=-1),
                      jnp.concatenate([at_i[0], at_i[1]], axis=-1)], axis=1)
    coef = coef.reshape(N_TILES, GROUPS_PER_TILE, 2, 2 * SSM_STATE).transpose(0, 2, 1, 3)
    return m_op, b_op, c_op_t, coef


def kernel(x, c, ctx, c_ctx, ada_w, ada_b, ln_g, ln_b, conv_w_in, conv_w, conv_w_out,
           ssm_w_in, ssm_lam_re, ssm_lam_im, ssm_log_step, ssm_b_re, ssm_b_im,
           ssm_c_re, ssm_c_im, ssm_d, ssm_w_glu, ssm_b_glu, ssm_w_out):
    bsz, seq, _ = x.shape
    ctx_len = ctx.shape[1]
    assert seq % 1024 == 0 and ctx_len % T_CHUNK == 0 and bsz < SUBLANES

    cs = jnp.concatenate([c, c_ctx[None], jnp.zeros((SUBLANES - bsz - 1, D_MODEL), _F32)], axis=0)
    mod = _ada_call(cs, ada_w, ada_b).reshape(DEPTH, SUBLANES, 3, D_MODEL)
    ctx_row = lambda i: bsz

    x2d = x.reshape(bsz * seq, D_MODEL)
    c2d = ctx.reshape(bsz * ctx_len, D_MODEL)
    ln_g3 = ln_g.reshape(DEPTH, 1, D_MODEL)
    ln_b3 = ln_b.reshape(DEPTH, 1, D_MODEL)

    w_in0 = _cast_bf16_call(conv_w_in, 0)
    w_out0 = _cast_bf16_call(conv_w_out, 0)
    conv_tb = 1024
    x1 = _conv_layer_call(x2d, mod, w_in0, conv_w, w_out0, ln_g3, ln_b3, layer=0, conv_index=0,
                          mod_row=lambda i: i // (seq // conv_tb), tb=conv_tb, tokens_per_batch=seq,
                          period=GRID_W, n_horizontal=E_CONV // 2, halo=True, name="conv_layer_x")
    c1 = _conv_layer_call(c2d, mod, w_in0, conv_w, w_out0, ln_g3, ln_b3, layer=0, conv_index=0,
                          mod_row=ctx_row, tb=ctx_len, tokens_per_batch=ctx_len,
                          period=ctx_len, n_horizontal=E_CONV, halo=False, name="conv_layer_ctx")

    w_in1 = _cast_bf16_call(ssm_w_in, 0)
    inproj_tb = 1024
    ug = _s5_inproj_call(x1, mod, w_in1, layer=1, mod_row=lambda i: i // (seq // inproj_tb),
                         tb=inproj_tb, name="s5_inproj_x")
    ucg = _s5_inproj_call(c1, mod, w_in1, layer=1, mod_row=ctx_row, tb=ctx_len, name="s5_inproj_ctx")
    ucg = ucg.reshape(N_GROUPS, bsz, ctx_len // T_CHUNK, TP)

    m_op, b_op, c_op, coef = _s5_operators(ssm_lam_re[0], ssm_lam_im[0], ssm_log_step[0],
                                           ssm_b_re[0], ssm_b_im[0], ssm_c_re[0], ssm_c_im[0],
                                           ssm_d[0])
    yg = _s5_core_call(ug, ucg, m_op, b_op, c_op, coef, n_batch=bsz)

    out = _s5_out_call(x1, yg, mod, w_in1, _cast_bf16_call(ssm_w_glu, 0),
                       ssm_b_glu.reshape(-1, 1, E_SSM), _cast_bf16_call(ssm_w_out, 0), ln_g3, ln_b3,
                       layer=1, tb=512, tokens_per_batch=seq)
    return out.reshape(bsz, seq, D_MODEL)
```

```python
import functools
import math

import jax
import jax.numpy as jnp
import numpy as np
from jax import lax
from jax.experimental import pallas as pl
from jax.experimental.pallas import tpu as pltpu

D_MODEL = 1024
GRID_W = 64
E_CONV = 2048
E_SSM = 2048
SSM_GROUP = 16
N_GROUPS = E_SSM // SSM_GROUP
SSM_STATE = 64
LN_EPS = 1e-5
DEPTH = 2
DN_ALPHA = (2 * DEPTH) ** 0.25

LANES = 128
SUBLANES = 8
T_CHUNK = 16
TP = T_CHUNK * SSM_GROUP
GROUPS_PER_TILE = LANES // SSM_GROUP
N_TILES = E_SSM // LANES
T_PITCH = 24
VMEM_LIMIT = 56 * 1024 * 1024

_F32 = jnp.float32
_BF16 = jnp.bfloat16


def _dot(a, b):
    return jnp.dot(a, b, preferred_element_type=_F32)


def _silu(z):
    return z * jax.nn.sigmoid(z)


def _gelu_tanh(x):
    c = math.sqrt(2.0 / math.pi)
    return x * (0.5 * (1.0 + jnp.tanh(c * (x + 0.044715 * (x * x * x)))))


def _layernorm_rows(r, g, b):
    mu = jnp.mean(r, axis=-1, keepdims=True)
    d = r - mu
    var = jnp.mean(d * d, axis=-1, keepdims=True)
    return d * lax.rsqrt(var + LN_EPS) * g + b


def _const_spec(shape, index=None):
    index = (0,) * len(shape) if index is None else index
    return pl.BlockSpec(shape, lambda *_: index, pipeline_mode=pl.Buffered(1))


def _mod_spec(layer, row_of_step):
    return pl.BlockSpec((None, None, 3, D_MODEL), lambda *i: (layer, row_of_step(*i), 0, 0))


CAST_BLOCK_ELEMS = 512 * 1024


def _cast_body(w_ref, o_ref):
    o_ref[...] = w_ref[...].astype(_BF16)


def _cast_bf16_call(w3d, index):
    _, rows, cols = w3d.shape
    rb = min(rows, CAST_BLOCK_ELEMS // cols)
    return pl.pallas_call(
        _cast_body,
        grid=(rows // rb,),
        in_specs=[pl.BlockSpec((None, rb, cols), lambda i: (index, i, 0))],
        out_specs=pl.BlockSpec((rb, cols), lambda i: (i, 0)),
        out_shape=jax.ShapeDtypeStruct((rows, cols), _BF16),
        compiler_params=pltpu.CompilerParams(dimension_semantics=("arbitrary",)),
        name="cast_bf16",
    )(w3d)


def _ada_body(c_ref, w_ref, b_ref, o_ref):
    o_ref[0] = jnp.dot(_silu(c_ref[...]), w_ref[0], precision=lax.Precision.HIGHEST,
                       preferred_element_type=_F32) + b_ref[0]


def _ada_call(cs, ada_w, ada_b):
    depth = ada_w.shape[0]
    rows = cs.shape[0]
    return pl.pallas_call(
        _ada_body,
        grid=(depth, 3),
        in_specs=[
            pl.BlockSpec((rows, D_MODEL), lambda i, j: (0, 0)),
            pl.BlockSpec((1, D_MODEL, D_MODEL), lambda i, j: (i, 0, j)),
            pl.BlockSpec((1, 1, D_MODEL), lambda i, j: (i, 0, j)),
        ],
        out_specs=pl.BlockSpec((1, rows, D_MODEL), lambda i, j: (i, 0, j)),
        out_shape=jax.ShapeDtypeStruct((depth, rows, 3 * D_MODEL), _F32),
        compiler_params=pltpu.CompilerParams(dimension_semantics=("arbitrary", "arbitrary")),
        name="ada_mod",
    )(cs, ada_w, ada_b.reshape(depth, 1, 3 * D_MODEL))


CONV_CK = 256


def _conv_layer_body(*refs, tb, period, n_horizontal, blocks_per_batch, halo):
    if halo:
        x_ref, xp_ref, xn_ref, mod_ref, win_ref, cw_ref, wout_ref, g_ref, b_ref, o_ref = refs
    else:
        x_ref, mod_ref, win_ref, cw_ref, wout_ref, g_ref, b_ref, o_ref = refs
    sh = mod_ref[0:1, :]
    sc1 = 1.0 + mod_ref[1:2, :]
    gt = mod_ref[2:3, :]
    x = x_ref[...]
    h = (x * sc1 + sh).astype(_BF16)
    if halo:
        i = pl.program_id(0)
        jb = i % blocks_per_batch
        prev_ok = (jb != 0).astype(_F32)
        next_ok = (jb != blocks_per_batch - 1).astype(_F32)
        hp = ((xp_ref[...] * sc1 + sh) * prev_ok).astype(_BF16)
        hn = ((xn_ref[...] * sc1 + sh) * next_ok).astype(_BF16)
        h_ext = jnp.concatenate([hp, h, hn], axis=0)
    pos = lax.broadcasted_iota(jnp.int32, (tb, CONV_CK), 0) % period
    acc = jnp.zeros((tb, D_MODEL), _F32)
    for k in range(E_CONV // CONV_CK):
        c0 = k * CONV_CK
        bg = _dot(h, win_ref[:, c0:c0 + CONV_CK])
        z = _dot(h, win_ref[:, 3 * E_CONV + c0:3 * E_CONV + c0 + CONV_CK])
        w0 = cw_ref[0:1, c0:c0 + CONV_CK]
        w1 = cw_ref[1:2, c0:c0 + CONV_CK]
        w2 = cw_ref[2:3, c0:c0 + CONV_CK]
        if c0 < n_horizontal:
            u = _dot(h, win_ref[:, E_CONV + c0:E_CONV + c0 + CONV_CK]) * \
                _dot(h, win_ref[:, 2 * E_CONV + c0:2 * E_CONV + c0 + CONV_CK])
            um = jnp.where(pos != 0, pltpu.roll(u, 1, 0), 0.0)
            up = jnp.where(pos != period - 1, pltpu.roll(u, tb - 1, 0), 0.0)
        else:
            ue = _dot(h_ext, win_ref[:, E_CONV + c0:E_CONV + c0 + CONV_CK]) * \
                _dot(h_ext, win_ref[:, 2 * E_CONV + c0:2 * E_CONV + c0 + CONV_CK])
            um = ue[0:tb]
            u = ue[GRID_W:tb + GRID_W]
            up = ue[2 * GRID_W:tb + 2 * GRID_W]
        yc = um * w0 + u * w1 + up * w2
        gated = (bg * yc * _silu(z)).astype(_BF16)
        acc = acc + _dot(gated, wout_ref[c0:c0 + CONV_CK, :])
    r = DN_ALPHA * x + gt * acc
    o_ref[...] = _layernorm_rows(r, g_ref[...], b_ref[...])


def _conv_layer_call(x2d, mod, w_in, conv_w, w_out, ln_g, ln_b, *, layer, conv_index, mod_row, tb,
                     tokens_per_batch, period, n_horizontal, halo, name):
    nt = x2d.shape[0]
    bpb = tokens_per_batch // tb
    hb = tb // GRID_W
    n_hblocks = nt // GRID_W
    in_specs = [pl.BlockSpec((tb, D_MODEL), lambda i: (i, 0))]
    args = [x2d]
    if halo:
        in_specs += [
            pl.BlockSpec((GRID_W, D_MODEL), lambda i: (jnp.maximum(i * hb - 1, 0), 0)),
            pl.BlockSpec((GRID_W, D_MODEL), lambda i: (jnp.minimum((i + 1) * hb, n_hblocks - 1), 0)),
        ]
        args += [x2d, x2d]
    in_specs += [
        _mod_spec(layer, mod_row),
        _const_spec((D_MODEL, 4 * E_CONV)),
        _const_spec((None, 3, E_CONV), (conv_index, 0, 0)),
        _const_spec((E_CONV, D_MODEL)),
        _const_spec((None, 1, D_MODEL), (layer, 0, 0)),
        _const_spec((None, 1, D_MODEL), (layer, 0, 0)),
    ]
    args += [mod, w_in, conv_w, w_out, ln_g, ln_b]
    body = functools.partial(_conv_layer_body, tb=tb, period=period, n_horizontal=n_horizontal,
                             blocks_per_batch=bpb, halo=halo)
    return pl.pallas_call(
        body,
        grid=(nt // tb,),
        in_specs=in_specs,
        out_specs=pl.BlockSpec((tb, D_MODEL), lambda i: (i, 0)),
        out_shape=jax.ShapeDtypeStruct((nt, D_MODEL), _F32),
        compiler_params=pltpu.CompilerParams(dimension_semantics=("arbitrary",),
                                             vmem_limit_bytes=VMEM_LIMIT),
        name=name,
    )(*args)


def _block_transpose8(v, lane):
    for d in (4, 2, 1):
        s = d * SSM_GROUP
        lo = (lane & s) == 0
        nv = list(v)
        for i in range(GROUPS_PER_TILE):
            if i & d == 0:
                a, b = v[i], v[i + d]
                nv[i] = jnp.where(lo, a, pltpu.roll(b, s, 1))
                nv[i + d] = jnp.where(lo, pltpu.roll(a, LANES - s, 1), b)
        v = nv
    return v


INPROJ_NC = 256


def _s5_inproj_body(x_ref, mod_ref, w_ref, o_ref, t_ref, *, tb):
    cb = tb // T_CHUNK
    sh = mod_ref[0:1, :]
    sc1 = 1.0 + mod_ref[1:2, :]
    h = (x_ref[...] * sc1 + sh).astype(_BF16)
    rows = 16
    lane = lax.broadcasted_iota(jnp.int32, (rows, LANES), 1)
    for c in range(E_SSM // INPROJ_NC):
        u = _dot(h, w_ref[:, c * INPROJ_NC:(c + 1) * INPROJ_NC])
        for half in range(INPROJ_NC // LANES):
            j = c * (INPROJ_NC // LANES) + half
            uj = u[:, half * LANES:(half + 1) * LANES]
            for ch in range(cb):
                t_ref[j, ch * T_PITCH:ch * T_PITCH + T_CHUNK, :] = uj[ch * T_CHUNK:(ch + 1) * T_CHUNK]
            for c0 in range(0, cb, rows):
                r = [t_ref[j, pl.ds(c0 * T_PITCH + t, rows, stride=T_PITCH), :] for t in range(T_CHUNK)]
                for th in range(T_CHUNK // GROUPS_PER_TILE):
                    w = _block_transpose8(r[th * GROUPS_PER_TILE:(th + 1) * GROUPS_PER_TILE], lane)
                    for gg in range(GROUPS_PER_TILE):
                        o_ref[j * GROUPS_PER_TILE + gg, c0:c0 + rows, th * LANES:(th + 1) * LANES] = \
                            w[gg].astype(_BF16)


def _s5_inproj_call(x2d, mod, w_in, *, layer, mod_row, tb, name):
    nt = x2d.shape[0]
    cb = tb // T_CHUNK
    return pl.pallas_call(
        functools.partial(_s5_inproj_body, tb=tb),
        grid=(nt // tb,),
        in_specs=[
            pl.BlockSpec((tb, D_MODEL), lambda i: (i, 0)),
            _mod_spec(layer, mod_row),
            _const_spec((D_MODEL, E_SSM), (0, 0)),
        ],
        out_specs=pl.BlockSpec((N_GROUPS, cb, TP), lambda i: (0, i, 0)),
        out_shape=jax.ShapeDtypeStruct((N_GROUPS, nt // T_CHUNK, TP), _BF16),
        scratch_shapes=[pltpu.VMEM((N_TILES, cb * T_PITCH, LANES), _F32)],
        compiler_params=pltpu.CompilerParams(dimension_semantics=("arbitrary",),
                                             vmem_limit_bytes=VMEM_LIMIT),
        name=name,
    )(x2d, mod, w_in)


SLAB_PAD_ROWS = 16


def _cmul_add(ar, ai, hr, hi, sr, si):
    return ar * hr - ai * hi + sr, ar * hi + ai * hr + si


def _s5_core_body(u_ref, uc_ref, m_ref, bm_ref, cm_ref, coef_ref, y_ref,
                  s_ref, sc_ref, h_ref, *, n_chunks, n_ctx_chunks):
    half = SSM_STATE
    g8 = GROUPS_PER_TILE
    for gg in range(g8):
        s = _dot(u_ref[gg], bm_ref[gg])
        s_ref[0, pl.ds(gg, n_chunks, stride=g8), :] = s[:, 0:LANES]
        s_ref[1, pl.ds(gg, n_chunks, stride=g8), :] = s[:, LANES:]
        sc = _dot(uc_ref[gg, 0], bm_ref[gg])
        sc_ref[0, pl.ds(gg, n_ctx_chunks, stride=g8), :] = sc[:, 0:LANES]
        sc_ref[1, pl.ds(gg, n_ctx_chunks, stride=g8), :] = sc[:, LANES:]

    ar = coef_ref[0, 0]
    ai = coef_ref[0, 1]
    is_fwd = lax.broadcasted_iota(jnp.int32, (g8, LANES), 1) < half

    def load_s(src_ref, kf, kb):
        return (jnp.where(is_fwd, src_ref[0, pl.ds(kf, g8), :], src_ref[0, pl.ds(kb, g8), :]),
                jnp.where(is_fwd, src_ref[1, pl.ds(kf, g8), :], src_ref[1, pl.ds(kb, g8), :]))

    hr = jnp.zeros((g8, LANES), _F32)
    hi = jnp.zeros((g8, LANES), _F32)
    for k in range(n_ctx_chunks):
        hr, hi = _cmul_add(ar, ai, hr, hi, *load_s(sc_ref, k * g8, (n_ctx_chunks - 1 - k) * g8))

    def body(k, carry):
        hr, hi = carry
        kf = pl.multiple_of(k * g8, g8)
        kb = pl.multiple_of((n_chunks - 1 - k) * g8, g8)
        h_ref[0, pl.ds(kf, g8), 0:half] = hr[:, 0:half]
        h_ref[1, pl.ds(kf, g8), 0:half] = hi[:, 0:half]
        h_ref[0, pl.ds(kb, g8), half:] = hr[:, half:]
        h_ref[1, pl.ds(kb, g8), half:] = hi[:, half:]
        return _cmul_add(ar, ai, hr, hi, *load_s(s_ref, kf, kb))

    lax.fori_loop(0, n_chunks, body, (hr, hi), unroll=4)

    for gg in range(g8):
        hg = jnp.concatenate([h_ref[0, pl.ds(gg, n_chunks, stride=g8), :],
                              h_ref[1, pl.ds(gg, n_chunks, stride=g8), :]], axis=-1)
        y = _dot(u_ref[gg], m_ref[gg]) + lax.dot_general(
            hg.astype(_BF16), cm_ref[gg], (((1,), (1,)), ((), ())), preferred_element_type=_F32)
        y_ref[gg] = y.astype(_BF16)


def _s5_core_call(ug, ucg, m_op, b_op, c_op, coef, *, n_batch):
    g, nrow, _ = ug.shape
    n_chunks = nrow // n_batch
    n_ctx_chunks = ucg.shape[2]
    g8 = GROUPS_PER_TILE
    body = functools.partial(_s5_core_body, n_chunks=n_chunks, n_ctx_chunks=n_ctx_chunks)
    op_spec = pl.BlockSpec((g8, TP, TP), lambda j, b: (j, 0, 0))
    return pl.pallas_call(
        body,
        grid=(g // g8, n_batch),
        in_specs=[
            pl.BlockSpec((g8, n_chunks, TP), lambda j, b: (j, b, 0)),
            pl.BlockSpec((g8, 1, n_ctx_chunks, TP), lambda j, b: (j, b, 0, 0)),
            op_spec, op_spec, op_spec,
            pl.BlockSpec((1, 2, g8, LANES), lambda j, b: (j, 0, 0, 0)),
        ],
        out_specs=pl.BlockSpec((g8, n_chunks, TP), lambda j, b: (j, b, 0)),
        out_shape=jax.ShapeDtypeStruct((g, nrow, TP), _BF16),
        scratch_shapes=[
            pltpu.VMEM((2, n_chunks * g8 + SLAB_PAD_ROWS, LANES), _F32),
            pltpu.VMEM((2, n_ctx_chunks * g8 + SLAB_PAD_ROWS, LANES), _F32),
            pltpu.VMEM((2, n_chunks * g8 + SLAB_PAD_ROWS, LANES), _F32),
        ],
        compiler_params=pltpu.CompilerParams(dimension_semantics=("arbitrary", "arbitrary"),
                                             vmem_limit_bytes=VMEM_LIMIT),
        name="s5_core",
    )(ug, ucg, m_op, b_op, c_op, coef)


OUT_NC = 1024
STAGE_SLOTS = 2


def _s5_out_body(x_ref, y_ref, mod_ref, wz_ref, wglu_ref, bglu_ref, wout_ref, g_ref, b_ref, o_ref,
                 t_ref, act_ref, *, tb):
    cb = tb // T_CHUNK

    def finish():
        sh = mod_ref[0:1, :]
        sc1 = 1.0 + mod_ref[1:2, :]
        gt = mod_ref[2:3, :]
        x = x_ref[...]
        h = (x * sc1 + sh).astype(_BF16)
        acc = jnp.zeros((tb, D_MODEL), _F32)
        for nb in range(E_SSM // OUT_NC):
            c0 = nb * OUT_NC
            gl = _dot(act_ref[...], wglu_ref[:, c0:c0 + OUT_NC]) + bglu_ref[:, c0:c0 + OUT_NC]
            z = _dot(h, wz_ref[:, c0:c0 + OUT_NC])
            gated = act_ref[:, c0:c0 + OUT_NC].astype(_F32) * jax.nn.sigmoid(gl) * _silu(z)
            acc = acc + _dot(gated.astype(_BF16), wout_ref[c0:c0 + OUT_NC, :])
        r = DN_ALPHA * x + gt * acc
        o_ref[...] = _layernorm_rows(r, g_ref[...], b_ref[...])

    def stage():
        rows = 32
        lane = lax.broadcasted_iota(jnp.int32, (rows, LANES), 1)
        for j in range(N_TILES):
            slot = j % STAGE_SLOTS
            for c0 in range(0, cb, rows):
                for th in range(T_CHUNK // GROUPS_PER_TILE):
                    w = [_gelu_tanh(y_ref[j * GROUPS_PER_TILE + gg, c0:c0 + rows,
                                          th * LANES:(th + 1) * LANES].astype(_F32))
                         for gg in range(GROUPS_PER_TILE)]
                    r = _block_transpose8(w, lane)
                    for a in range(GROUPS_PER_TILE):
                        t_ref[slot, pl.ds(c0 * T_PITCH + th * GROUPS_PER_TILE + a, rows, stride=T_PITCH), :] = r[a]
            for ch in range(cb):
                act_ref[ch * T_CHUNK:(ch + 1) * T_CHUNK, j * LANES:(j + 1) * LANES] = \
                    t_ref[slot, ch * T_PITCH:ch * T_PITCH + T_CHUNK, :].astype(_BF16)

    @pl.when(pl.program_id(0) == 0)
    def _():
        stage()

    @pl.when(pl.program_id(0) > 0)
    def _():
        finish()
        stage()


def _s5_out_call(x2d, yg, mod, w_in, w_glu, b_glu, w_out, ln_g, ln_b, *, layer, tb, tokens_per_batch):
    nt = x2d.shape[0]
    bpb = tokens_per_batch // tb
    cb = tb // T_CHUNK
    nblk = nt // tb
    prev = lambda i: jnp.maximum(i - 1, 0)
    return pl.pallas_call(
        functools.partial(_s5_out_body, tb=tb),
        grid=(nblk + 1,),
        in_specs=[
            pl.BlockSpec((tb, D_MODEL), lambda i: (prev(i), 0)),
            pl.BlockSpec((N_GROUPS, cb, TP), lambda i: (0, jnp.minimum(i, nblk - 1), 0)),
            _mod_spec(layer, lambda i: prev(i) // bpb),
            _const_spec((D_MODEL, E_SSM), (0, 1)),
            _const_spec((E_SSM, E_SSM)),
            _const_spec((None, 1, E_SSM), (0, 0, 0)),
            _const_spec((E_SSM, D_MODEL)),
            _const_spec((None, 1, D_MODEL), (layer, 0, 0)),
            _const_spec((None, 1, D_MODEL), (layer, 0, 0)),
        ],
        out_specs=pl.BlockSpec((tb, D_MODEL), lambda i: (prev(i), 0)),
        out_shape=jax.ShapeDtypeStruct((nt, D_MODEL), _F32),
        scratch_shapes=[pltpu.VMEM((STAGE_SLOTS, cb * T_PITCH, LANES), _F32),
                        pltpu.VMEM((tb, E_SSM), _BF16)],
        compiler_params=pltpu.CompilerParams(dimension_semantics=("arbitrary",),
                                             vmem_limit_bytes=VMEM_LIMIT),
        name="s5_out",
    )(x2d, yg, mod, w_in, w_glu, b_glu, w_out, ln_g, ln_b)


def _s5_expand_body(pb_ref, bt_ref, pc_ref, cr_ref, kc_ref, m_ref, b_ref, c_ref):
    t = T_CHUNK
    for gg in range(GROUPS_PER_TILE):
        bt_r, bt_i = bt_ref[gg, 0], bt_ref[gg, 1]
        cr_r, cr_i = cr_ref[gg, 0], cr_ref[gg, 1]
        taps = kc_ref[gg]
        for s in range(t):
            rows = slice(s * SSM_GROUP, (s + 1) * SSM_GROUP)
            p_r, p_i = pb_ref[gg, 0, s:s + 1, :], pb_ref[gg, 1, s:s + 1, :]
            b_ref[gg, rows, 0:LANES] = (p_r * bt_r - p_i * bt_i).astype(_BF16)
            b_ref[gg, rows, LANES:] = (p_r * bt_i + p_i * bt_r).astype(_BF16)
            p_r, p_i = pc_ref[gg, 0, s:s + 1, :], pc_ref[gg, 1, s:s + 1, :]
            c_ref[gg, rows, 0:LANES] = (p_r * cr_r - p_i * cr_i).astype(_BF16)
            c_ref[gg, rows, LANES:] = (-(p_r * cr_i + p_i * cr_r)).astype(_BF16)
            w0 = (t - 1 - s) * SSM_GROUP
            m_ref[gg, rows, :] = taps[:, w0:w0 + TP].astype(_BF16)


def _s5_expand_call(pb, bt, pc, cr, kc):
    g8 = GROUPS_PER_TILE
    tab = pl.BlockSpec((g8, 2, T_CHUNK, LANES), lambda j: (j, 0, 0, 0))
    op = pl.BlockSpec((g8, TP, TP), lambda j: (j, 0, 0))
    op_shape = jax.ShapeDtypeStruct((N_GROUPS, TP, TP), _BF16)
    return pl.pallas_call(
        _s5_expand_body,
        grid=(N_TILES,),
        in_specs=[tab, tab, tab, tab, pl.BlockSpec((g8, SSM_GROUP, 2 * TP), lambda j: (j, 0, 0))],
        out_specs=[op, op, op],
        out_shape=[op_shape, op_shape, op_shape],
        compiler_params=pltpu.CompilerParams(dimension_semantics=("arbitrary",)),
        name="s5_expand",
    )(pb, bt, pc, cr, kc)


def _s5_operators(lam_re, lam_im, log_step, b_re, b_im, c_re, c_im, d):
    t = T_CHUNK
    dt = jnp.exp(log_step)[..., None]
    rate = lam_re * dt
    theta = lam_im * dt
    a_r = jnp.exp(rate) * jnp.cos(theta)
    a_i = jnp.exp(rate) * jnp.sin(theta)
    q_r, q_i = a_r - 1.0, a_i
    den = lam_re * lam_re + lam_im * lam_im
    f_r = (q_r * lam_re + q_i * lam_im) / den
    f_i = (q_i * lam_re - q_r * lam_im) / den
    bb_r = f_r[..., None] * b_re - f_i[..., None] * b_im
    bb_i = f_r[..., None] * b_im + f_i[..., None] * b_re

    def powers(k_fwd, k_bwd, k_last):
        k = np.stack([k_fwd, k_bwd]).astype(np.float32)
        k = k[:, None, None, :] if k_last else k[:, None, :, None]
        ra = rate[..., None] if k_last else rate[:, :, None, :]
        th = theta[..., None] if k_last else theta[:, :, None, :]
        mag = jnp.exp(k * ra)
        return mag * jnp.cos(k * th), mag * jnp.sin(k * th)

    ar_t = np.arange(t)

    def pack(x_r, x_i):
        return jnp.stack([jnp.concatenate([x_r[0], x_r[1]], axis=-1),
                          jnp.concatenate([x_i[0], x_i[1]], axis=-1)], axis=1)

    pb = pack(*powers(t - 1 - ar_t, ar_t, False))
    pc = pack(*powers(ar_t + 1, t - ar_t, False))
    bt = pack(bb_r.transpose(0, 1, 3, 2), bb_i.transpose(0, 1, 3, 2))
    cr = pack(c_re, c_im)

    pk_r, pk_i = powers(ar_t, t - 1 - ar_t, True)
    abk_r = bb_r[..., None] * pk_r[:, :, :, None, :] - bb_i[..., None] * pk_i[:, :, :, None, :]
    abk_i = bb_r[..., None] * pk_i[:, :, :, None, :] + bb_i[..., None] * pk_r[:, :, :, None, :]
    abk_r = abk_r.reshape(2, N_GROUPS, SSM_STATE, TP).astype(_BF16)
    abk_i = abk_i.reshape(2, N_GROUPS, SSM_STATE, TP).astype(_BF16)
    kq = (jnp.einsum('rgnx,rgpn->rgxp', abk_r, c_re.astype(_BF16), preferred_element_type=_F32)
          - jnp.einsum('rgnx,rgpn->rgxp', abk_i, c_im.astype(_BF16), preferred_element_type=_F32))
    kq = kq.reshape(2, N_GROUPS, SSM_GROUP, t, SSM_GROUP)
    skip = (jnp.eye(SSM_GROUP, dtype=_F32)[None] * d.reshape(N_GROUPS, 1, SSM_GROUP))[:, :, None, :]
    center = kq[1][:, :, t - 1:t] + kq[0][:, :, 0:1] + skip
    kc = jnp.concatenate([kq[1][:, :, :t - 1], center, kq[0][:, :, 1:],
                          jnp.zeros_like(center)], axis=2)
    kc = kc.reshape(N_GROUPS, SSM_GROUP, 2 * TP)
    m_op, b_op, c_op_t = _s5_expand_call(pb, bt, pc, cr, kc)

    at_r, at_i = powers(np.array([t]), np.array([t]), False)
    coef = jnp.stack([jnp.concatenate([at_r[0], at_r[1]], axis=-1),
                      jnp.concatenate([at_i[0], at_i[1]], axis=-1)], axis=1)
    coef = coef.reshape(N_TILES, GROUPS_PER_TILE, 2, 2 * SSM_STATE).transpose(0, 2, 1, 3)
    return m_op, b_op, c_op_t, coef


def kernel(x, c, ctx, c_ctx, ada_w, ada_b, ln_g, ln_b, conv_w_in, conv_w, conv_w_out,
           ssm_w_in, ssm_lam_re, ssm_lam_im, ssm_log_step, ssm_b_re, ssm_b_im,
           ssm_c_re, ssm_c_im, ssm_d, ssm_w_glu, ssm_b_glu, ssm_w_out):
    bsz, seq, _ = x.shape
    ctx_len = ctx.shape[1]
    assert seq % 1024 == 0 and ctx_len % T_CHUNK == 0 and bsz < SUBLANES

    cs = jnp.concatenate([c, c_ctx[None], jnp.zeros((SUBLANES - bsz - 1, D_MODEL), _F32)], axis=0)
    mod = _ada_call(cs, ada_w, ada_b).reshape(DEPTH, SUBLANES, 3, D_MODEL)
    ctx_row = lambda i: bsz

    x2d = x.reshape(bsz * seq, D_MODEL)
    c2d = ctx.reshape(bsz * ctx_len, D_MODEL)
    ln_g3 = ln_g.reshape(DEPTH, 1, D_MODEL)
    ln_b3 = ln_b.reshape(DEPTH, 1, D_MODEL)

    w_in0 = _cast_bf16_call(conv_w_in, 0)
    w_out0 = _cast_bf16_call(conv_w_out, 0)
    conv_tb = 1024
    x1 = _conv_layer_call(x2d, mod, w_in0, conv_w, w_out0, ln_g3, ln_b3, layer=0, conv_index=0,
                          mod_row=lambda i: i // (seq // conv_tb), tb=conv_tb, tokens_per_batch=seq,
                          period=GRID_W, n_horizontal=E_CONV // 2, halo=True, name="conv_layer_x")
    c1 = _conv_layer_call(c2d, mod, w_in0, conv_w, w_out0, ln_g3, ln_b3, layer=0, conv_index=0,
                          mod_row=ctx_row, tb=ctx_len, tokens_per_batch=ctx_len,
                          period=ctx_len, n_horizontal=E_CONV, halo=False, name="conv_layer_ctx")

    w_in1 = _cast_bf16_call(ssm_w_in, 0)
    inproj_tb = 1024
    ug = _s5_inproj_call(x1, mod, w_in1, layer=1, mod_row=lambda i: i // (seq // inproj_tb),
                         tb=inproj_tb, name="s5_inproj_x")
    ucg = _s5_inproj_call(c1, mod, w_in1, layer=1, mod_row=ctx_row, tb=ctx_len, name="s5_inproj_ctx")
    ucg = ucg.reshape(N_GROUPS, bsz, ctx_len // T_CHUNK, TP)

    m_op, b_op, c_op, coef = _s5_operators(ssm_lam_re[0], ssm_lam_im[0], ssm_log_step[0],
                                           ssm_b_re[0], ssm_b_im[0], ssm_c_re[0], ssm_c_im[0],
                                           ssm_d[0])
    yg = _s5_core_call(ug, ucg, m_op, b_op, c_op, coef, n_batch=bsz)

    out = _s5_out_call(x1, yg, mod, w_in1, _cast_bf16_call(ssm_w_glu, 0),
                       ssm_b_glu.reshape(-1, 1, E_SSM), _cast_bf16_call(ssm_w_out, 0), ln_g3, ln_b3,
                       layer=1, tb=512, tokens_per_batch=seq)
    return out.reshape(bsz, seq, D_MODEL)
```
